```python
import math
import jax, jax.numpy as jnp
from jax import lax
import numpy as np

D_MODEL = 1024
BATCH = 4
SEQ = 8192
DEPTH = 4
DEC_BATCH = 8
DEC_SEQ = 2048
PAST_LEN = 128

GRID_W = 64
Q_BLOCK = 128
EPS = 1e-6
D_FF = 2816
D_PLE = 256
ROPE_BASE = 10000.0

A_HEADS = 8
A_KV_HEADS = 2
A_HEAD_DIM = 64
B_HEADS = 8
B_NOPE = 64
B_ROPE = 32
B_VDIM = 64
B_Q_RANK = 256
B_KV_RANK = 128
C_HEADS = 8
C_HEAD_DIM = 64

N_EVEN = (DEPTH + 1) // 2
N_ODD = DEPTH // 2

AB_SPLITS = (A_HEADS * A_HEAD_DIM, A_KV_HEADS * A_HEAD_DIM, A_KV_HEADS * A_HEAD_DIM,
             B_Q_RANK, B_KV_RANK, B_ROPE)
AB_IN = sum(AB_SPLITS)
AB_OUT = A_HEADS * A_HEAD_DIM + B_HEADS * B_VDIM
C_IN = 3 * C_HEADS * 2 * C_HEAD_DIM
C_OUT = C_HEADS * 2 * C_HEAD_DIM

kernel_name = "hybrid_gqa_mla_diffattn_macaron_encoder"


def rmsnorm(x, g):
    xf = x.astype(jnp.float32)
    y = xf * lax.rsqrt(jnp.mean(xf * xf, axis=-1, keepdims=True) + EPS)
    return (y * g.astype(jnp.float32)).astype(x.dtype)


def swiglu(x, wg, wu, wd):
    return (jax.nn.silu(x @ wg) * (x @ wu)) @ wd


def rope(x, pos):
    d = x.shape[-1]
    half = d // 2
    inv = ROPE_BASE ** (-jnp.arange(half, dtype=jnp.float32) * (2.0 / d))
    ang = pos.astype(jnp.float32)[:, None] * inv[None, :]
    cos, sin = jnp.cos(ang), jnp.sin(ang)
    xf = x.astype(jnp.float32)
    x1, x2 = xf[..., :half], xf[..., half:]
    return jnp.concatenate([x1 * cos - x2 * sin, x2 * cos + x1 * sin], axis=-1).astype(x.dtype)


def axial_rope(x, row, col):
    h = x.shape[-1] // 2
    return jnp.concatenate([rope(x[..., :h], row), rope(x[..., h:], col)], axis=-1)


def to_heads(t, n_heads):
    b, s, _ = t.shape
    return t.reshape(b, s, n_heads, -1).transpose(0, 2, 1, 3)


def from_heads(t):
    b, h, s, d = t.shape
    return t.transpose(0, 2, 1, 3).reshape(b, s, h * d)


def sweep_query_blocks(block_fn, n_q):
    starts = jnp.arange(n_q // Q_BLOCK, dtype=jnp.int32) * Q_BLOCK
    out = lax.map(block_fn, starts)
    nb, b, h, qb, dv = out.shape
    return jnp.moveaxis(out, 0, 2).reshape(b, h, nb * qb, dv)


def alibi_slopes():
    return 2.0 ** (-8.0 * jnp.arange(1, C_HEADS + 1, dtype=jnp.float32) / C_HEADS)


def mixer_gqa_mla(h, w_in, a_qn, a_kn, b_qn, b_wuq, b_kvn, b_wukv, w_out, row, col, tpos):
    b, s, _ = h.shape
    z = h @ w_in
    cuts = []
    acc = 0
    for w in AB_SPLITS[:-1]:
        acc += w
        cuts.append(acc)
    za_q, za_k, za_v, zb_cq, zb_ckv, zb_kr = jnp.split(z, cuts, axis=-1)

    qa = axial_rope(rmsnorm(to_heads(za_q, A_HEADS), a_qn), row, col)
    ka = axial_rope(rmsnorm(to_heads(za_k, A_KV_HEADS), a_kn), row, col)
    va = to_heads(za_v, A_KV_HEADS)
    qa = qa.reshape(b, A_KV_HEADS, A_HEADS // A_KV_HEADS, s, A_HEAD_DIM)
    scale_a = A_HEAD_DIM ** -0.5

    def blk_a(start):
        qblk = lax.dynamic_slice_in_dim(qa, start, Q_BLOCK, axis=3)
        sc = jnp.einsum('bkgqd,bksd->bkgqs', qblk, ka).astype(jnp.float32) * scale_a
        pr = jax.nn.softmax(sc, axis=-1).astype(va.dtype)
        o = jnp.einsum('bkgqs,bksd->bkgqd', pr, va)
        return o.reshape(b, A_HEADS, Q_BLOCK, A_HEAD_DIM)

    oa = sweep_query_blocks(blk_a, s)

    qb_all = to_heads(rmsnorm(zb_cq, b_qn) @ b_wuq, B_HEADS)
    q_nope = qb_all[..., :B_NOPE]
    q_rot = rope(qb_all[..., B_NOPE:], tpos)
    kv = to_heads(rmsnorm(zb_ckv, b_kvn) @ b_wukv, B_HEADS)
    k_nope = kv[..., :B_NOPE]
    vb = kv[..., B_NOPE:]
    k_rot = rope(zb_kr, tpos)
    scale_b = (B_NOPE + B_ROPE) ** -0.5

    def blk_b(start):
        qn = lax.dynamic_slice_in_dim(q_nope, start, Q_BLOCK, axis=2)
        qr = lax.dynamic_slice_in_dim(q_rot, start, Q_BLOCK, axis=2)
        sc = (jnp.einsum('bhqd,bhsd->bhqs', qn, k_nope)
              + jnp.einsum('bhqr,bsr->bhqs', qr, k_rot)).astype(jnp.float32) * scale_b
        pr = jax.nn.softmax(sc, axis=-1).astype(vb.dtype)
        return jnp.einsum('bhqs,bhsd->bhqd', pr, vb)

    ob = sweep_query_blocks(blk_b, s)

    o = jnp.concatenate([from_heads(oa), from_heads(ob)], axis=-1)
    return o @ w_out


def mixer_diff(h, w_in, lq1, lk1, lq2, lk2, sub_g, w_out, lambda_init, slopes):
    b, s, _ = h.shape
    q, k, v = jnp.split(h @ w_in, 3, axis=-1)
    q = to_heads(q, C_HEADS)
    k = to_heads(k, C_HEADS)
    v = to_heads(v, C_HEADS)
    q1, q2 = q[..., :C_HEAD_DIM], q[..., C_HEAD_DIM:]
    k1, k2 = k[..., :C_HEAD_DIM], k[..., C_HEAD_DIM:]
    lam = (jnp.exp(jnp.sum(lq1.astype(jnp.float32) * lk1.astype(jnp.float32)))
           - jnp.exp(jnp.sum(lq2.astype(jnp.float32) * lk2.astype(jnp.float32)))
           + lambda_init)
    kpos = jnp.arange(s, dtype=jnp.float32)
    scale = C_HEAD_DIM ** -0.5

    def blk(start):
        q1b = lax.dynamic_slice_in_dim(q1, start, Q_BLOCK, axis=2)
        q2b = lax.dynamic_slice_in_dim(q2, start, Q_BLOCK, axis=2)
        qpos = start.astype(jnp.float32) + jnp.arange(Q_BLOCK, dtype=jnp.float32)
        bias = -slopes[:, None, None] * jnp.abs(qpos[:, None] - kpos[None, :])[None]
        s1 = jnp.einsum('bhqd,bhsd->bhqs', q1b, k1).astype(jnp.float32) * scale + bias
        s2 = jnp.einsum('bhqd,bhsd->bhqs', q2b, k2).astype(jnp.float32) * scale + bias
        a = jax.nn.softmax(s1, axis=-1) - lam * jax.nn.softmax(s2, axis=-1)
        return jnp.einsum('bhqs,bhsd->bhqd', a.astype(v.dtype), v)

    o = sweep_query_blocks(blk, s)
    o = rmsnorm(o, sub_g) * (1.0 - lambda_init)
    return from_heads(o) @ w_out


def trunk(x, p, ffn1_norm, ffn1_wg, ffn1_wu, ffn1_wd, mix_norm,
          ab_w_in, a_q_norm, a_k_norm, b_q_norm, b_w_uq, b_kv_norm, b_w_ukv, ab_w_out,
          c_w_in, c_lambda_q1, c_lambda_k1, c_lambda_q2, c_lambda_k2, c_sub_norm, c_w_out,
          ffn2_norm, ffn2_wg, ffn2_wu, ffn2_wd, ple_norm, ple_w_gate, ple_w_proj, final_norm):
    s = x.shape[1]
    rows = s // GRID_W
    row = jnp.repeat(jnp.arange(rows, dtype=jnp.int32), GRID_W)
    col = jnp.tile(jnp.arange(GRID_W, dtype=jnp.int32), rows)
    tpos = jnp.arange(s, dtype=jnp.int32)
    slopes = alibi_slopes()
    for i in range(DEPTH):
        x = x + 0.5 * swiglu(rmsnorm(x, ffn1_norm[i]), ffn1_wg[i], ffn1_wu[i], ffn1_wd[i])
        h = rmsnorm(x, mix_norm[i])
        if i % 2 == 0:
            e = i // 2
            x = x + mixer_gqa_mla(h, ab_w_in[e], a_q_norm[e], a_k_norm[e], b_q_norm[e], b_w_uq[e],
                                  b_kv_norm[e], b_w_ukv[e], ab_w_out[e], row, col, tpos)
        else:
            o = i // 2
            lambda_init = 0.8 - 0.6 * math.exp(-0.3 * i)
            x = x + mixer_diff(h, c_w_in[o], c_lambda_q1[o], c_lambda_k1[o], c_lambda_q2[o],
                               c_lambda_k2[o], c_sub_norm[o], c_w_out[o], lambda_init, slopes)
        x = x + 0.5 * swiglu(rmsnorm(x, ffn2_norm[i]), ffn2_wg[i], ffn2_wu[i], ffn2_wd[i])
        gate = jax.nn.sigmoid(rmsnorm(x, ple_norm[i]) @ ple_w_gate[i])
        x = x + gate * (p[i] @ ple_w_proj[i])
    return rmsnorm(x, final_norm)


def setup_inputs(seed: int = 0) -> dict:
    key = jax.random.key(seed)
    ks = iter(jax.random.split(key, 64))
    f32 = jnp.float32

    def w(shape, fan_in):
        return jax.random.normal(next(ks), shape, f32) * (fan_in ** -0.5)

    def gain(shape):
        return 1.0 + 0.1 * jax.random.normal(next(ks), shape, f32)

    def small(shape):
        return 0.1 * jax.random.normal(next(ks), shape, f32)

    return {
        "x_prompt": jax.random.normal(next(ks), (BATCH, SEQ, D_MODEL), f32),
        "x_sample": jax.random.normal(next(ks), (DEC_BATCH, DEC_SEQ, D_MODEL), f32),
        "p_prompt": jax.random.normal(next(ks), (DEPTH, BATCH, SEQ, D_PLE), f32),
        "p_sample": jax.random.normal(next(ks), (DEPTH, DEC_BATCH, DEC_SEQ, D_PLE), f32),
        "ffn1_norm": gain((DEPTH, D_MODEL)),
        "ffn1_wg": w((DEPTH, D_MODEL, D_FF), D_MODEL),
        "ffn1_wu": w((DEPTH, D_MODEL, D_FF), D_MODEL),
        "ffn1_wd": w((DEPTH, D_FF, D_MODEL), D_FF),
        "mix_norm": gain((DEPTH, D_MODEL)),
        "ab_w_in": w((N_EVEN, D_MODEL, AB_IN), D_MODEL),
        "a_q_norm": gain((N_EVEN, A_HEAD_DIM)),
        "a_k_norm": gain((N_EVEN, A_HEAD_DIM)),
        "b_q_norm": gain((N_EVEN, B_Q_RANK)),
        "b_w_uq": w((N_EVEN, B_Q_RANK, B_HEADS * (B_NOPE + B_ROPE)), B_Q_RANK),
        "b_kv_norm": gain((N_EVEN, B_KV_RANK)),
        "b_w_ukv": w((N_EVEN, B_KV_RANK, B_HEADS * (B_NOPE + B_VDIM)), B_KV_RANK),
        "ab_w_out": w((N_EVEN, AB_OUT, D_MODEL), AB_OUT),
        "c_w_in": w((N_ODD, D_MODEL, C_IN), D_MODEL),
        "c_lambda_q1": small((N_ODD, C_HEAD_DIM)),
        "c_lambda_k1": small((N_ODD, C_HEAD_DIM)),
        "c_lambda_q2": small((N_ODD, C_HEAD_DIM)),
        "c_lambda_k2": small((N_ODD, C_HEAD_DIM)),
        "c_sub_norm": gain((N_ODD, 2 * C_HEAD_DIM)),
        "c_w_out": w((N_ODD, C_OUT, D_MODEL), C_OUT),
        "ffn2_norm": gain((DEPTH, D_MODEL)),
        "ffn2_wg": w((DEPTH, D_MODEL, D_FF), D_MODEL),
        "ffn2_wu": w((DEPTH, D_MODEL, D_FF), D_MODEL),
        "ffn2_wd": w((DEPTH, D_FF, D_MODEL), D_FF),
        "ple_norm": gain((DEPTH, D_MODEL)),
        "ple_w_gate": w((DEPTH, D_MODEL, D_MODEL), D_MODEL),
        "ple_w_proj": w((DEPTH, D_PLE, D_MODEL), D_PLE),
        "final_norm": gain((D_MODEL,)),
    }


def reference(x_prompt, x_sample, p_prompt, p_sample, ffn1_norm, ffn1_wg, ffn1_wu, ffn1_wd,
              mix_norm, ab_w_in, a_q_norm, a_k_norm, b_q_norm, b_w_uq, b_kv_norm, b_w_ukv,
              ab_w_out, c_w_in, c_lambda_q1, c_lambda_k1, c_lambda_q2, c_lambda_k2, c_sub_norm,
              c_w_out, ffn2_norm, ffn2_wg, ffn2_wu, ffn2_wd, ple_norm, ple_w_gate, ple_w_proj,
              final_norm):
    y_prompt = trunk(x_prompt, p_prompt, ffn1_norm, ffn1_wg, ffn1_wu, ffn1_wd, mix_norm,
                     ab_w_in, a_q_norm, a_k_norm, b_q_norm, b_w_uq, b_kv_norm, b_w_ukv, ab_w_out,
                     c_w_in, c_lambda_q1, c_lambda_k1, c_lambda_q2, c_lambda_k2, c_sub_norm, c_w_out,
                     ffn2_norm, ffn2_wg, ffn2_wu, ffn2_wd, ple_norm, ple_w_gate, ple_w_proj, final_norm)
    y_sample = trunk(x_sample, p_sample, ffn1_norm, ffn1_wg, ffn1_wu, ffn1_wd, mix_norm,
                     ab_w_in, a_q_norm, a_k_norm, b_q_norm, b_w_uq, b_kv_norm, b_w_ukv, ab_w_out,
                     c_w_in, c_lambda_q1, c_lambda_k1, c_lambda_q2, c_lambda_k2, c_sub_norm, c_w_out,
                     ffn2_norm, ffn2_wg, ffn2_wu, ffn2_wd, ple_norm, ple_w_gate, ple_w_proj, final_norm)
    return (y_prompt, y_sample)
```

```python
import functools
import math

import numpy as np
import jax
import jax.numpy as jnp
from jax import lax
from jax.experimental import pallas as pl
from jax.experimental.pallas import tpu as pltpu

D_MODEL = 1024
DEPTH = 4
GRID_W = 64
EPS = 1e-6
D_FF = 2816
D_PLE = 256
ROPE_BASE = 10000.0
A_HEADS = 8
A_KV_HEADS = 2
A_HEAD_DIM = 64
B_HEADS = 8
B_NOPE = 64
B_ROPE = 32
B_VDIM = 64
B_Q_RANK = 256
B_KV_RANK = 128
C_HEADS = 8
C_HEAD_DIM = 64

LANES = 128
VMEM_LIMIT = 56 * 1024 * 1024
NEG_BIG = -1e30

F32 = jnp.float32
BF16 = jnp.bfloat16


def _cparams(n_axes):
    return pltpu.CompilerParams(
        dimension_semantics=("parallel",) * n_axes, vmem_limit_bytes=VMEM_LIMIT)


def _const_spec(shape):
    nd = len(shape)
    return pl.BlockSpec(shape, lambda *_: (0,) * nd, pipeline_mode=pl.Buffered(1))


def _rms(x, g):
    return x * lax.rsqrt(jnp.mean(x * x, axis=-1, keepdims=True) + EPS) * g


def _sigmoid(x):
    return 1.0 / (1.0 + jnp.exp(-x))


def _ffn_body(x_ref, g_ref, wg_ref, wu_ref, wd_ref, o_ref):
    x = x_ref[...]
    xn = _rms(x, g_ref[...]).astype(BF16)
    g = jnp.dot(xn, wg_ref[...], preferred_element_type=F32)
    u = jnp.dot(xn, wu_ref[...], preferred_element_type=F32)
    a = (g * _sigmoid(g) * u).astype(BF16)
    y = jnp.dot(a, wd_ref[...], preferred_element_type=F32)
    o_ref[...] = x + 0.5 * y


def _ffn(x, g, wg, wu, wd, tm):
    t, d = x.shape
    f = wg.shape[1]
    return pl.pallas_call(
        _ffn_body,
        grid=(t // tm,),
        in_specs=[
            pl.BlockSpec((tm, d), lambda i: (i, 0)),
            _const_spec((1, d)),
            _const_spec((d, f)),
            _const_spec((d, f)),
            _const_spec((f, d)),
        ],
        out_specs=pl.BlockSpec((tm, d), lambda i: (i, 0)),
        out_shape=jax.ShapeDtypeStruct((t, d), F32),
        compiler_params=_cparams(1),
        name="ffn",
    )(x, g, wg, wu, wd)


def _slab(ref_or_val, i):
    return ref_or_val[:, LANES * i:LANES * (i + 1)]


def _inproj_even_body(x_ref, g_ref, w1_ref, cosa_ref, sina_ref, cosb_ref, sinb_ref,
                      gq_ref, gqp_ref, gk_ref, gkp_ref, bqn_ref, wuq_ref, bkvn_ref, wukv_ref,
                      qa_ref, ka_ref, va_ref, qb_ref, kb_ref, vb_ref):
    h = _rms(x_ref[...], g_ref[...]).astype(BF16)
    z = jnp.dot(h, w1_ref[...], preferred_element_type=F32)
    cosa, sina = cosa_ref[...], sina_ref[...]
    cosb, sinb = cosb_ref[...], sinb_ref[...]

    scale_a = A_HEAD_DIM ** -0.5
    cq, sq = cosa * (gq_ref[...] * scale_a), sina * (gqp_ref[...] * scale_a)
    ck, sk = cosa * gk_ref[...], sina * gkp_ref[...]
    o_main, o_rot = 0, A_HEADS
    for i in range(A_HEADS):
        zz, zr = _slab(z, o_main + i), _slab(z, o_rot + i)
        r = lax.rsqrt(jnp.sum(zz * zz, axis=-1, keepdims=True) * (1.0 / A_HEAD_DIM) + EPS)
        qa_ref[:, LANES * i:LANES * (i + 1)] = ((zz * cq + zr * sq) * r).astype(BF16)
    o_main, o_rot = 2 * A_HEADS, 2 * A_HEADS + A_KV_HEADS
    for i in range(A_KV_HEADS):
        zz, zr = _slab(z, o_main + i), _slab(z, o_rot + i)
        r = lax.rsqrt(jnp.sum(zz * zz, axis=-1, keepdims=True) * (1.0 / A_HEAD_DIM) + EPS)
        ka_ref[:, LANES * i:LANES * (i + 1)] = ((zz * ck + zr * sk) * r).astype(BF16)
    o_v = 2 * A_HEADS + 2 * A_KV_HEADS
    for i in range(A_KV_HEADS):
        va_ref[:, LANES * i:LANES * (i + 1)] = _slab(z, o_v + i).astype(BF16)

    c0 = LANES * (o_v + A_KV_HEADS)
    scale_b = (B_NOPE + B_ROPE) ** -0.5
    cqn = _rms(z[:, c0:c0 + B_Q_RANK], bqn_ref[...]).astype(BF16)
    zq = jnp.dot(cqn, wuq_ref[...], preferred_element_type=F32)
    for i in range(B_HEADS):
        qb_ref[:, LANES * i:LANES * (i + 1)] = (
            (_slab(zq, i) * cosb + _slab(zq, B_HEADS + i) * sinb) * scale_b).astype(BF16)
    c1 = c0 + B_Q_RANK
    ckvn = _rms(z[:, c1:c1 + B_KV_RANK], bkvn_ref[...]).astype(BF16)
    zkv = jnp.dot(ckvn, wukv_ref[...], preferred_element_type=F32)
    c2 = c1 + B_KV_RANK
    kr = z[:, c2:c2 + LANES] * cosb + z[:, c2 + LANES:c2 + 2 * LANES] * sinb
    for i in range(B_HEADS):
        kb_ref[:, LANES * i:LANES * (i + 1)] = (_slab(zkv, i) + kr).astype(BF16)
        vb_ref[:, LANES * i:LANES * (i + 1)] = _slab(zkv, B_HEADS + i).astype(BF16)


def _inproj_even(x, g, w1, tabs, gq, gqp, gk, gkp, bqn, wuq, bkvn, wukv, seq, tm):
    t, d = x.shape
    nt = seq // tm
    tab_spec = pl.BlockSpec((tm, LANES), lambda i: (i % nt, 0))
    row = lambda i: (i, 0)
    widths = (A_HEADS, A_KV_HEADS, A_KV_HEADS, B_HEADS, B_HEADS, B_HEADS)
    return pl.pallas_call(
        _inproj_even_body,
        grid=(t // tm,),
        in_specs=[pl.BlockSpec((tm, d), row), _const_spec((1, d)), _const_spec(w1.shape),
                  tab_spec, tab_spec, tab_spec, tab_spec,
                  _const_spec((1, LANES)), _const_spec((1, LANES)),
                  _const_spec((1, LANES)), _const_spec((1, LANES)),
                  _const_spec((1, B_Q_RANK)), _const_spec(wuq.shape),
                  _const_spec((1, B_KV_RANK)), _const_spec(wukv.shape)],
        out_specs=[pl.BlockSpec((tm, LANES * n), row) for n in widths],
        out_shape=[jax.ShapeDtypeStruct((t, LANES * n), BF16) for n in widths],
        compiler_params=_cparams(1),
        name="inproj_even",
    )(x, g, w1, *tabs, gq, gqp, gk, gkp, bqn, wuq, bkvn, wukv)


def _inproj_odd_body(x_ref, g_ref, w_ref, q_ref, k_ref, v_ref):
    h = _rms(x_ref[...], g_ref[...]).astype(BF16)
    z = jnp.dot(h, w_ref[...], preferred_element_type=F32)
    nq = 2 * C_HEADS * LANES
    nk = C_HEADS * LANES
    q_ref[...] = (z[:, :nq] * (C_HEAD_DIM ** -0.5)).astype(BF16)
    k_ref[...] = z[:, nq:nq + nk].astype(BF16)
    v_ref[...] = z[:, nq + nk:].astype(BF16)


def _inproj_odd(x, g, w, tm):
    t, d = x.shape
    row = lambda i: (i, 0)
    widths = (2 * C_HEADS, C_HEADS, C_HEADS)
    return pl.pallas_call(
        _inproj_odd_body,
        grid=(t // tm,),
        in_specs=[pl.BlockSpec((tm, d), row), _const_spec((1, d)), _const_spec(w.shape)],
        out_specs=[pl.BlockSpec((tm, LANES * n), row) for n in widths],
        out_shape=[jax.ShapeDtypeStruct((t, LANES * n), BF16) for n in widths],
        compiler_params=_cparams(1),
        name="inproj_odd",
    )(x, g, w)


def _outproj_body(*refs):
    x_ref, o_ref = refs[0], refs[-1]
    acc = x_ref[...]
    pairs = refs[1:-1]
    for a_ref, w_ref in zip(pairs[0::2], pairs[1::2]):
        acc = acc + jnp.dot(a_ref[...], w_ref[...], preferred_element_type=F32)
    o_ref[...] = acc


def _outproj(x, pairs, tm):
    t, d = x.shape
    row = lambda i: (i, 0)
    in_specs = [pl.BlockSpec((tm, d), row)]
    args = [x]
    for a, w in pairs:
        in_specs += [pl.BlockSpec((tm, a.shape[1]), row), _const_spec(w.shape)]
        args += [a, w]
    return pl.pallas_call(
        _outproj_body,
        grid=(t // tm,),
        in_specs=in_specs,
        out_specs=pl.BlockSpec((tm, d), row),
        out_shape=jax.ShapeDtypeStruct((t, d), F32),
        compiler_params=_cparams(1),
        name="outproj",
    )(*args)


def _ple_body(x_ref, p_ref, g_ref, wgate_ref, wproj_ref, fg_ref, o_ref, *, final):
    x = x_ref[...]
    xn = _rms(x, g_ref[...]).astype(BF16)
    gate = _sigmoid(jnp.dot(xn, wgate_ref[...], preferred_element_type=F32))
    proj = jnp.dot(p_ref[...].astype(BF16), wproj_ref[...], preferred_element_type=F32)
    y = x + gate * proj
    if final:
        y = _rms(y, fg_ref[...])
    o_ref[...] = y


def _ple(x, p, g, wgate, wproj, fg, final, tm):
    t, d = x.shape
    row = lambda i: (i, 0)
    return pl.pallas_call(
        functools.partial(_ple_body, final=final),
        grid=(t // tm,),
        in_specs=[pl.BlockSpec((tm, d), row), pl.BlockSpec((tm, p.shape[1]), row),
                  _const_spec((1, d)), _const_spec(wgate.shape), _const_spec(wproj.shape),
                  _const_spec((1, d))],
        out_specs=pl.BlockSpec((tm, d), row),
        out_shape=jax.ShapeDtypeStruct((t, d), F32),
        compiler_params=_cparams(1),
        name="ple",
    )(x, p, g, wgate, wproj, fg)


def _attn_body(*refs, groups, tq, tk, seq, diff, lambda_init):
    if diff:
        slopes_ref, q_ref, k_ref, v_ref, lq1_ref, lk1_ref, lq2_ref, lk2_ref, subg_ref, o_ref = refs
    else:
        q_ref, k_ref, v_ref, o_ref = refs
    rows = groups * tq
    qblk = q_ref[0]
    if groups > 1:
        q = jnp.concatenate([_slab(qblk, g) for g in range(groups)], axis=0)
    else:
        q = qblk
    if diff:
        slope = slopes_ref[pl.program_id(1)]
        q0 = pl.program_id(2) * tq
        rel = (lax.broadcasted_iota(jnp.int32, (tq, tk), 0)
               - lax.broadcasted_iota(jnp.int32, (tq, tk), 1))

    def step(j, carry):
        m, l, acc = carry
        k0 = pl.multiple_of(j * tk, tk)
        k = k_ref[0, pl.ds(k0, tk), :]
        v = v_ref[0, pl.ds(k0, tk), :]
        s = lax.dot_general(q, k, (((1,), (1,)), ((), ())), preferred_element_type=F32)
        if diff:
            bias = -slope * jnp.abs(rel + (q0 - k0)).astype(F32)
            s = s + jnp.concatenate([bias] * groups, axis=0)
        m_new = jnp.maximum(m, jnp.max(s, axis=1, keepdims=True))
        alpha = jnp.exp(m - m_new)
        p = jnp.exp(s - m_new)
        l = alpha * l + jnp.sum(p, axis=1, keepdims=True)
        acc = alpha * acc + jnp.dot(p.astype(BF16), v, preferred_element_type=F32)
        return m_new, l, acc

    init = (jnp.full((rows, 1), NEG_BIG, F32), jnp.zeros((rows, 1), F32),
            jnp.zeros((rows, LANES), F32))
    _, l, acc = lax.fori_loop(0, seq // tk, step, init)
    o = acc / l
    if diff:
        lam = (jnp.exp(jnp.sum(lq1_ref[...] * lk1_ref[...], keepdims=True))
               - jnp.exp(jnp.sum(lq2_ref[...] * lk2_ref[...], keepdims=True)) + lambda_init)
        od = o[:tq] - lam * o[tq:]
        o_ref[0] = (_rms(od, subg_ref[...]) * (1.0 - lambda_init)).astype(BF16)
    else:
        for g in range(groups):
            o_ref[0, :, LANES * g:LANES * (g + 1)] = o[g * tq:(g + 1) * tq].astype(BF16)


def _attention(q, k, v, groups, tq, tk, diff_args=None, lambda_init=0.0):
    b, s, _ = q.shape
    nkv = k.shape[2] // LANES
    tq, tk = min(tq, s), min(tk, s)
    diff = diff_args is not None
    q_spec = pl.BlockSpec((1, tq, groups * LANES), lambda bi, h, i: (bi, i, h))
    kv_spec = pl.BlockSpec((1, s, LANES), lambda bi, h, i: (bi, 0, h))
    in_specs = [q_spec, kv_spec, kv_spec]
    args = [q, k, v]
    out_w = 1 if diff else groups
    if diff:
        slopes, lq1, lk1, lq2, lk2, subg = diff_args
        in_specs = ([pl.BlockSpec(memory_space=pltpu.SMEM)] + in_specs
                    + [_const_spec((1, C_HEAD_DIM))] * 4 + [_const_spec((1, LANES))])
        args = [slopes] + args + [lq1, lk1, lq2, lk2, subg]
    return pl.pallas_call(
        functools.partial(_attn_body, groups=groups, tq=tq, tk=tk, seq=s, diff=diff,
                          lambda_init=lambda_init),
        grid=(b, nkv, s // tq),
        in_specs=in_specs,
        out_specs=pl.BlockSpec((1, tq, out_w * LANES), lambda bi, h, i: (bi, i, h)),
        out_shape=jax.ShapeDtypeStruct((b, s, nkv * out_w * LANES), BF16),
        compiler_params=_cparams(3),
        name="attn_diff" if diff else f"attn_g{groups}",
    )(*args)


def _pad_slabs(w, n, width, offset=0):
    kdim = w.shape[0]
    w = w.reshape(kdim, n, width)
    w = jnp.pad(w, ((0, 0), (0, 0), (offset, LANES - width - offset)))
    return w.reshape(kdim, n * LANES)


def _rot_cols(w, n, perm, sign):
    kdim = w.shape[0]
    width = perm.shape[0]
    w = w.reshape(kdim, n, width)[:, :, perm] * sign
    return w.reshape(kdim, n * width)


_HALF = B_ROPE // 2
_PERM32 = np.concatenate([np.arange(_HALF, 2 * _HALF), np.arange(_HALF)])
_SIGN32 = np.concatenate([-np.ones(_HALF), np.ones(_HALF)]).astype(np.float32)
_PERM64 = np.concatenate([_PERM32, _PERM32 + 2 * _HALF])
_SIGN64 = np.concatenate([_SIGN32, _SIGN32])


def _rope_tables(seq):
    t = jnp.arange(seq, dtype=jnp.int32)
    inv = ROPE_BASE ** (-jnp.arange(_HALF, dtype=F32) * (2.0 / (2 * _HALF)))

    def cs(pos):
        ang = pos.astype(F32)[:, None] * inv[None, :]
        return jnp.cos(ang), jnp.sin(ang)

    cr, sr = cs(t // GRID_W)
    cc, sc = cs(t % GRID_W)
    ct, st = cs(t)
    za = jnp.zeros((seq, LANES - A_HEAD_DIM), F32)
    cosa = jnp.concatenate([cr, cr, cc, cc, za], axis=1)
    sina = jnp.concatenate([sr, sr, sc, sc, za], axis=1)
    zb = jnp.zeros((seq, LANES - B_NOPE - B_ROPE), F32)
    cosb = jnp.concatenate([jnp.ones((seq, B_NOPE), F32), ct, ct, zb], axis=1)
    sinb = jnp.concatenate([jnp.zeros((seq, B_NOPE), F32), st, st, zb], axis=1)
    return cosa, sina, cosb, sinb


def _lane_row(g, perm=None):
    g = g.astype(F32)
    if perm is not None:
        g = g[perm]
    return jnp.pad(g, (0, LANES - g.shape[0])).reshape(1, LANES)


def _prep_even(w_in, a_qn, a_kn, b_qn, b_wuq, b_kvn, b_wukv, w_out):
    na, nk = A_HEADS * A_HEAD_DIM, A_KV_HEADS * A_HEAD_DIM
    wq, wk, wv = w_in[:, :na], w_in[:, na:na + nk], w_in[:, na + nk:na + 2 * nk]
    c = na + 2 * nk
    wcq, wckv = w_in[:, c:c + B_Q_RANK], w_in[:, c + B_Q_RANK:c + B_Q_RANK + B_KV_RANK]
    wkr = w_in[:, c + B_Q_RANK + B_KV_RANK:]
    w1 = jnp.concatenate([
        _pad_slabs(wq, A_HEADS, A_HEAD_DIM),
        _pad_slabs(_rot_cols(wq, A_HEADS, _PERM64, _SIGN64), A_HEADS, A_HEAD_DIM),
        _pad_slabs(wk, A_KV_HEADS, A_HEAD_DIM),
        _pad_slabs(_rot_cols(wk, A_KV_HEADS, _PERM64, _SIGN64), A_KV_HEADS, A_HEAD_DIM),
        _pad_slabs(wv, A_KV_HEADS, A_HEAD_DIM),
        wcq, wckv,
        _pad_slabs(wkr, 1, B_ROPE, offset=B_NOPE),
        _pad_slabs(_rot_cols(wkr, 1, _PERM32, _SIGN32), 1, B_ROPE, offset=B_NOPE),
    ], axis=1).astype(BF16)
    dq = B_NOPE + B_ROPE
    uq = b_wuq.reshape(B_Q_RANK, B_HEADS, dq)
    uq_rot = uq[:, :, B_NOPE:][:, :, _PERM32] * _SIGN32
    wuq = jnp.concatenate([
        _pad_slabs(uq.reshape(B_Q_RANK, B_HEADS * dq), B_HEADS, dq),
        _pad_slabs(uq_rot.reshape(B_Q_RANK, B_HEADS * B_ROPE), B_HEADS, B_ROPE, offset=B_NOPE),
    ], axis=1).astype(BF16)
    ukv = b_wukv.reshape(B_KV_RANK, B_HEADS, B_NOPE + B_VDIM)
    wukv = jnp.concatenate([
        _pad_slabs(ukv[:, :, :B_NOPE].reshape(B_KV_RANK, -1), B_HEADS, B_NOPE),
        _pad_slabs(ukv[:, :, B_NOPE:].reshape(B_KV_RANK, -1), B_HEADS, B_VDIM),
    ], axis=1).astype(BF16)
    wo_a = jnp.pad(w_out[:na].reshape(A_HEADS, A_HEAD_DIM, D_MODEL),
                   ((0, 0), (0, LANES - A_HEAD_DIM), (0, 0))).reshape(A_HEADS * LANES, D_MODEL)
    wo_b = jnp.pad(w_out[na:].reshape(B_HEADS, B_VDIM, D_MODEL),
                   ((0, 0), (0, LANES - B_VDIM), (0, 0))).reshape(B_HEADS * LANES, D_MODEL)
    return dict(
        w1=w1, wuq=wuq, wukv=wukv, wo_a=wo_a.astype(BF16), wo_b=wo_b.astype(BF16),
        gq=_lane_row(a_qn), gqp=_lane_row(a_qn, _PERM64),
        gk=_lane_row(a_kn), gkp=_lane_row(a_kn, _PERM64),
        bqn=b_qn.reshape(1, -1), bkvn=b_kvn.reshape(1, -1))


def _prep_odd(w_in, w_out):
    n = C_HEADS * 2 * C_HEAD_DIM
    wq = w_in[:, :n].reshape(D_MODEL, C_HEADS, 2, C_HEAD_DIM)
    zeros = jnp.zeros_like(wq[:, :, 0])
    q1 = jnp.concatenate([wq[:, :, 0], zeros], axis=-1)
    q2 = jnp.concatenate([zeros, wq[:, :, 1]], axis=-1)
    wq2 = jnp.stack([q1, q2], axis=2).reshape(D_MODEL, 2 * n)
    w = jnp.concatenate([wq2, w_in[:, n:]], axis=1).astype(BF16)
    return dict(w=w, wo=w_out.astype(BF16))


def _alibi_slopes():
    return 2.0 ** (-8.0 * jnp.arange(1, C_HEADS + 1, dtype=F32) / C_HEADS)


def _trunk(x, p, lw, final_norm, tm_ffn=512, tm_proj=256, tq=256, tk=512):
    b, s, d = x.shape
    t = b * s
    tm_ffn, tm_proj = min(tm_ffn, t), min(tm_proj, s)
    x = x.reshape(t, d)
    tabs = _rope_tables(s)
    slopes = _alibi_slopes()
    fg = final_norm.reshape(1, d)
    for i in range(DEPTH):
        w = lw[i]
        x = _ffn(x, w["ffn1_norm"], w["ffn1_wg"], w["ffn1_wu"], w["ffn1_wd"], tm_ffn)
        if i % 2 == 0:
            m = w["mix"]
            qa, ka, va, qb, kb, vb = _inproj_even(
                x, w["mix_norm"], m["w1"], tabs, m["gq"], m["gqp"], m["gk"], m["gkp"],
                m["bqn"], m["wuq"], m["bkvn"], m["wukv"], s, tm_proj)
            r3 = lambda a: a.reshape(b, s, a.shape[1])
            oa = _attention(r3(qa), r3(ka), r3(va), A_HEADS // A_KV_HEADS, tq, tk)
            ob = _attention(r3(qb), r3(kb), r3(vb), 1, 2 * tq, tk)
            x = _outproj(x, [(oa.reshape(t, -1), m["wo_a"]), (ob.reshape(t, -1), m["wo_b"])], tm_ffn)
        else:
            m = w["mix"]
            lambda_init = 0.8 - 0.6 * math.exp(-0.3 * i)
            q, k, v = _inproj_odd(x, w["mix_norm"], m["w"], tm_proj)
            r3 = lambda a: a.reshape(b, s, a.shape[1])
            oc = _attention(r3(q), r3(k), r3(v), 2, tq, tk,
                            diff_args=(slopes, w["lq1"], w["lk1"], w["lq2"], w["lk2"], w["subg"]),
                            lambda_init=lambda_init)
            x = _outproj(x, [(oc.reshape(t, -1), m["wo"])], tm_ffn)
        x = _ffn(x, w["ffn2_norm"], w["ffn2_wg"], w["ffn2_wu"], w["ffn2_wd"], tm_ffn)
        x = _ple(x, p[i].reshape(t, -1), w["ple_norm"], w["ple_w_gate"], w["ple_w_proj"], fg,
                 i == DEPTH - 1, tm_ffn)
    return x.reshape(b, s, d)


def _layer_weights(ffn1_norm, ffn1_wg, ffn1_wu, ffn1_wd, mix_norm, ab_w_in, a_q_norm, a_k_norm,
                   b_q_norm, b_w_uq, b_kv_norm, b_w_ukv, ab_w_out, c_w_in, c_lambda_q1,
                   c_lambda_k1, c_lambda_q2, c_lambda_k2, c_sub_norm, c_w_out, ffn2_norm, ffn2_wg,
                   ffn2_wu, ffn2_wd, ple_norm, ple_w_gate, ple_w_proj):
    layers = []
    for i in range(DEPTH):
        w = dict(
            ffn1_norm=ffn1_norm[i].reshape(1, -1), ffn1_wg=ffn1_wg[i].astype(BF16),
            ffn1_wu=ffn1_wu[i].astype(BF16), ffn1_wd=ffn1_wd[i].astype(BF16),
            mix_norm=mix_norm[i].reshape(1, -1),
            ffn2_norm=ffn2_norm[i].reshape(1, -1), ffn2_wg=ffn2_wg[i].astype(BF16),
            ffn2_wu=ffn2_wu[i].astype(BF16), ffn2_wd=ffn2_wd[i].astype(BF16),
            ple_norm=ple_norm[i].reshape(1, -1), ple_w_gate=ple_w_gate[i].astype(BF16),
            ple_w_proj=ple_w_proj[i].astype(BF16))
        j = i // 2
        if i % 2 == 0:
            w["mix"] = _prep_even(ab_w_in[j], a_q_norm[j], a_k_norm[j], b_q_norm[j], b_w_uq[j],
                                  b_kv_norm[j], b_w_ukv[j], ab_w_out[j])
        else:
            w["mix"] = _prep_odd(c_w_in[j], c_w_out[j])
            w["lq1"] = c_lambda_q1[j].reshape(1, -1)
            w["lk1"] = c_lambda_k1[j].reshape(1, -1)
            w["lq2"] = c_lambda_q2[j].reshape(1, -1)
            w["lk2"] = c_lambda_k2[j].reshape(1, -1)
            w["subg"] = c_sub_norm[j].reshape(1, -1)
        layers.append(w)
    return layers


def kernel(x_prompt, x_sample, p_prompt, p_sample, ffn1_norm, ffn1_wg, ffn1_wu, ffn1_wd, mix_norm, ab_w_in, a_q_norm, a_k_norm, b_q_norm, b_w_uq, b_kv_norm, b_w_ukv, ab_w_out, c_w_in, c_lambda_q1, c_lambda_k1, c_lambda_q2, c_lambda_k2, c_sub_norm, c_w_out, ffn2_norm, ffn2_wg, ffn2_wu, ffn2_wd, ple_norm, ple_w_gate, ple_w_proj, final_norm):
    lw = _layer_weights(ffn1_norm, ffn1_wg, ffn1_wu, ffn1_wd, mix_norm, ab_w_in, a_q_norm,
                        a_k_norm, b_q_norm, b_w_uq, b_kv_norm, b_w_ukv, ab_w_out, c_w_in,
                        c_lambda_q1, c_lambda_k1, c_lambda_q2, c_lambda_k2, c_sub_norm, c_w_out,
                        ffn2_norm, ffn2_wg, ffn2_wu, ffn2_wd, ple_norm, ple_w_gate, ple_w_proj)
    y_prompt = _trunk(x_prompt, p_prompt, lw, final_norm)
    y_sample = _trunk(x_sample, p_sample, lw, final_norm)
    return (y_prompt, y_sample)
```

```python
import functools
import math

import numpy as np
import jax
import jax.numpy as jnp
from jax import lax
from jax.experimental import pallas as pl
from jax.experimental.pallas import tpu as pltpu

D_MODEL = 1024
DEPTH = 4
GRID_W = 64
EPS = 1e-6
D_FF = 2816
D_PLE = 256
ROPE_BASE = 10000.0
A_HEADS = 8
A_KV_HEADS = 2
A_HEAD_DIM = 64
B_HEADS = 8
B_NOPE = 64
B_ROPE = 32
B_VDIM = 64
B_Q_RANK = 256
B_KV_RANK = 128
C_HEADS = 8
C_HEAD_DIM = 64

LANES = 128
VMEM_LIMIT = 56 * 1024 * 1024
NEG_BIG = -1e30
LOG2E = math.log2(math.e)
ONES_PAD = 16

F32 = jnp.float32
BF16 = jnp.bfloat16


def _cparams(n_axes):
    return pltpu.CompilerParams(
        dimension_semantics=("parallel",) * n_axes, vmem_limit_bytes=VMEM_LIMIT)


def _const_spec(shape):
    nd = len(shape)
    return pl.BlockSpec(shape, lambda *_: (0,) * nd, pipeline_mode=pl.Buffered(1))


def _rms(x, g):
    return x * lax.rsqrt(jnp.mean(x * x, axis=-1, keepdims=True) + EPS) * g


def _sigmoid(x):
    return 1.0 / (1.0 + jnp.exp(-x))


def _ffn_body(x_ref, g_ref, wg_ref, wu_ref, wd_ref, o_ref):
    x = x_ref[...]
    xn = _rms(x, g_ref[...]).astype(BF16)
    g = jnp.dot(xn, wg_ref[...], preferred_element_type=F32)
    u = jnp.dot(xn, wu_ref[...], preferred_element_type=F32)
    a = (g * _sigmoid(g) * u).astype(BF16)
    y = jnp.dot(a, wd_ref[...], preferred_element_type=F32)
    o_ref[...] = x + 0.5 * y


def _ffn(x, g, wg, wu, wd, tm):
    t, d = x.shape
    f = wg.shape[1]
    return pl.pallas_call(
        _ffn_body,
        grid=(t // tm,),
        in_specs=[
            pl.BlockSpec((tm, d), lambda i: (i, 0)),
            _const_spec((1, d)),
            _const_spec((d, f)),
            _const_spec((d, f)),
            _const_spec((f, d)),
        ],
        out_specs=pl.BlockSpec((tm, d), lambda i: (i, 0)),
        out_shape=jax.ShapeDtypeStruct((t, d), F32),
        compiler_params=_cparams(1),
        name="ffn",
    )(x, g, wg, wu, wd)


def _slab(ref_or_val, i):
    return ref_or_val[:, LANES * i:LANES * (i + 1)]


def _dot_nt(a, b):
    return lax.dot_general(a, b, (((1,), (1,)), ((), ())), preferred_element_type=F32)


def _values_t(wvt, h, dv):
    vt = _dot_nt(wvt, h)
    row = lax.broadcasted_iota(jnp.int32, vt.shape, 0)
    return jnp.where(row % (dv + ONES_PAD) == dv, 1.0, vt).astype(BF16)


def _inproj_even_body(x_ref, g_ref, w1_ref, wvat_ref, cosa_ref, sina_ref, cosb_ref, sinb_ref,
                      gq_ref, gqp_ref, gk_ref, gkp_ref, bqn_ref, wuq_ref, bkvn_ref, wukv_ref,
                      wvbt_ref, qa_ref, ka_ref, vat_ref, qb_ref, kb_ref, vbt_ref):
    h = _rms(x_ref[...], g_ref[...]).astype(BF16)
    z = jnp.dot(h, w1_ref[...], preferred_element_type=F32)
    vat_ref[...] = _values_t(wvat_ref[...], h, A_HEAD_DIM)
    cosa, sina = cosa_ref[...], sina_ref[...]
    cosb, sinb = cosb_ref[...], sinb_ref[...]

    scale_a = A_HEAD_DIM ** -0.5 * LOG2E
    cq, sq = cosa * (gq_ref[...] * scale_a), sina * (gqp_ref[...] * scale_a)
    ck, sk = cosa * gk_ref[...], sina * gkp_ref[...]
    o_main, o_rot = 0, A_HEADS
    for i in range(A_HEADS):
        zz, zr = _slab(z, o_main + i), _slab(z, o_rot + i)
        r = lax.rsqrt(jnp.sum(zz * zz, axis=-1, keepdims=True) * (1.0 / A_HEAD_DIM) + EPS)
        qa_ref[:, LANES * i:LANES * (i + 1)] = ((zz * cq + zr * sq) * r).astype(BF16)
    o_main, o_rot = 2 * A_HEADS, 2 * A_HEADS + A_KV_HEADS
    for i in range(A_KV_HEADS):
        zz, zr = _slab(z, o_main + i), _slab(z, o_rot + i)
        r = lax.rsqrt(jnp.sum(zz * zz, axis=-1, keepdims=True) * (1.0 / A_HEAD_DIM) + EPS)
        ka_ref[:, LANES * i:LANES * (i + 1)] = ((zz * ck + zr * sk) * r).astype(BF16)

    c0 = LANES * (2 * A_HEADS + 2 * A_KV_HEADS)
    scale_b = (B_NOPE + B_ROPE) ** -0.5 * LOG2E
    cqn = _rms(z[:, c0:c0 + B_Q_RANK], bqn_ref[...]).astype(BF16)
    zq = jnp.dot(cqn, wuq_ref[...], preferred_element_type=F32)
    for i in range(B_HEADS):
        qb_ref[:, LANES * i:LANES * (i + 1)] = (
            (_slab(zq, i) * cosb + _slab(zq, B_HEADS + i) * sinb) * scale_b).astype(BF16)
    c1 = c0 + B_Q_RANK
    ckvn = _rms(z[:, c1:c1 + B_KV_RANK], bkvn_ref[...]).astype(BF16)
    zk = jnp.dot(ckvn, wukv_ref[...], preferred_element_type=F32)
    vbt_ref[...] = _values_t(wvbt_ref[...], ckvn, B_VDIM)
    c2 = c1 + B_KV_RANK
    kr = z[:, c2:c2 + LANES] * cosb + z[:, c2 + LANES:c2 + 2 * LANES] * sinb
    for i in range(B_HEADS):
        kb_ref[:, LANES * i:LANES * (i + 1)] = (_slab(zk, i) + kr).astype(BF16)


def _inproj_even(x, g, w1, wvat, tabs, gq, gqp, gk, gkp, bqn, wuq, bkvn, wukv, wvbt, seq, tm):
    t, d = x.shape
    nt = seq // tm
    tab_spec = pl.BlockSpec((tm, LANES), lambda i: (i % nt, 0))
    row = lambda i: (i, 0)
    col = lambda i: (0, i)
    outs = ((A_HEADS * LANES, row), (A_KV_HEADS * LANES, row), (wvat.shape[0], col),
            (B_HEADS * LANES, row), (B_HEADS * LANES, row), (wvbt.shape[0], col))
    return pl.pallas_call(
        _inproj_even_body,
        grid=(t // tm,),
        in_specs=[pl.BlockSpec((tm, d), row), _const_spec((1, d)), _const_spec(w1.shape),
                  _const_spec(wvat.shape),
                  tab_spec, tab_spec, tab_spec, tab_spec,
                  _const_spec((1, LANES)), _const_spec((1, LANES)),
                  _const_spec((1, LANES)), _const_spec((1, LANES)),
                  _const_spec((1, B_Q_RANK)), _const_spec(wuq.shape),
                  _const_spec((1, B_KV_RANK)), _const_spec(wukv.shape),
                  _const_spec(wvbt.shape)],
        out_specs=[pl.BlockSpec((tm, n) if m is row else (n, tm), m) for n, m in outs],
        out_shape=[jax.ShapeDtypeStruct((t, n) if m is row else (n, t), BF16)
                   for n, m in outs],
        compiler_params=_cparams(1),
        name="inproj_even",
    )(x, g, w1, wvat, *tabs, gq, gqp, gk, gkp, bqn, wuq, bkvn, wukv, wvbt)


def _inproj_odd_body(x_ref, g_ref, w_ref, wvt_ref, q_ref, k_ref, vt_ref):
    h = _rms(x_ref[...], g_ref[...]).astype(BF16)
    z = jnp.dot(h, w_ref[...], preferred_element_type=F32)
    nq = 2 * C_HEADS * LANES
    q_ref[...] = (z[:, :nq] * (C_HEAD_DIM ** -0.5 * LOG2E)).astype(BF16)
    k_ref[...] = z[:, nq:].astype(BF16)
    vt_ref[...] = _values_t(wvt_ref[...], h, 2 * C_HEAD_DIM)


def _inproj_odd(x, g, w, wvt, tm):
    t, d = x.shape
    row = lambda i: (i, 0)
    return pl.pallas_call(
        _inproj_odd_body,
        grid=(t // tm,),
        in_specs=[pl.BlockSpec((tm, d), row), _const_spec((1, d)), _const_spec(w.shape),
                  _const_spec(wvt.shape)],
        out_specs=[pl.BlockSpec((tm, 2 * C_HEADS * LANES), row),
                   pl.BlockSpec((tm, C_HEADS * LANES), row),
                   pl.BlockSpec((wvt.shape[0], tm), lambda i: (0, i))],
        out_shape=[jax.ShapeDtypeStruct((t, 2 * C_HEADS * LANES), BF16),
                   jax.ShapeDtypeStruct((t, C_HEADS * LANES), BF16),
                   jax.ShapeDtypeStruct((wvt.shape[0], t), BF16)],
        compiler_params=_cparams(1),
        name="inproj_odd",
    )(x, g, w, wvt)


def _outproj_body(*refs):
    x_ref, o_ref = refs[0], refs[-1]
    acc = x_ref[...]
    pairs = refs[1:-1]
    for a_ref, w_ref in zip(pairs[0::2], pairs[1::2]):
        acc = acc + jnp.dot(a_ref[...], w_ref[...], preferred_element_type=F32)
    o_ref[...] = acc


def _outproj(x, pairs, tm):
    t, d = x.shape
    row = lambda i: (i, 0)
    in_specs = [pl.BlockSpec((tm, d), row)]
    args = [x]
    for a, w in pairs:
        in_specs += [pl.BlockSpec((tm, a.shape[1]), row), _const_spec(w.shape)]
        args += [a, w]
    return pl.pallas_call(
        _outproj_body,
        grid=(t // tm,),
        in_specs=in_specs,
        out_specs=pl.BlockSpec((tm, d), row),
        out_shape=jax.ShapeDtypeStruct((t, d), F32),
        compiler_params=_cparams(1),
        name="outproj",
    )(*args)


def _ple_body(x_ref, p_ref, g_ref, wgate_ref, wproj_ref, fg_ref, o_ref, *, final):
    x = x_ref[...]
    xn = _rms(x, g_ref[...]).astype(BF16)
    gate = _sigmoid(jnp.dot(xn, wgate_ref[...], preferred_element_type=F32))
    proj = jnp.dot(p_ref[...].astype(BF16), wproj_ref[...], preferred_element_type=F32)
    y = x + gate * proj
    if final:
        y = _rms(y, fg_ref[...])
    o_ref[...] = y


def _ple(x, p, g, wgate, wproj, fg, final, tm):
    t, d = x.shape
    row = lambda i: (i, 0)
    return pl.pallas_call(
        functools.partial(_ple_body, final=final),
        grid=(t // tm,),
        in_specs=[pl.BlockSpec((tm, d), row), pl.BlockSpec((tm, p.shape[1]), row),
                  _const_spec((1, d)), _const_spec(wgate.shape), _const_spec(wproj.shape),
                  _const_spec((1, d))],
        out_specs=pl.BlockSpec((tm, d), row),
        out_shape=jax.ShapeDtypeStruct((t, d), F32),
        compiler_params=_cparams(1),
        name="ple",
    )(x, p, g, wgate, wproj, fg)


def _attn_body(*refs, groups, tq, tk, unroll, seq, diff, lambda_init):
    if diff:
        (slopes_ref, q_ref, k_ref, vt_ref, lq1_ref, lk1_ref, lq2_ref, lk2_ref, subg_ref, o_ref,
         qs_ref, s_ref, p_ref, alpha_ref, m_ref, acc_ref) = refs
    else:
        q_ref, k_ref, vt_ref, o_ref, qs_ref, s_ref, p_ref, alpha_ref, m_ref, acc_ref = refs
    dv = vt_ref.shape[0] - ONES_PAD
    for g in range(groups):
        qs_ref[g * tq:(g + 1) * tq, :] = q_ref[0, :, LANES * g:LANES * (g + 1)]
    m_ref[...] = jnp.full(m_ref.shape, NEG_BIG, F32)
    acc_ref[...] = jnp.zeros(acc_ref.shape, F32)
    if diff:
        slope = slopes_ref[pl.program_id(1)] * LOG2E
        q0 = pl.program_id(2) * tq
        rel = (lax.broadcasted_iota(jnp.int32, (tk, tq), 1)
               - lax.broadcasted_iota(jnp.int32, (tk, tq), 0))

    def scores(c, slot):
        k0 = pl.multiple_of(c * tk, tk)
        s_ref[slot] = _dot_nt(k_ref[0, pl.ds(k0, tk), :], qs_ref[...])

    def softmax(c, slot):
        s = s_ref[slot]
        if diff:
            bias = -slope * jnp.abs(rel + (q0 - c * tk)).astype(F32)
            s = s + jnp.concatenate([bias] * groups, axis=1)
        m = m_ref[...]
        m_new = jnp.maximum(m, jnp.max(s, axis=0, keepdims=True))
        m_ref[...] = m_new
        alpha_ref[slot] = jnp.exp2(m - m_new)
        p_ref[slot] = jnp.exp2(s - m_new).astype(BF16)

    def values(c, slot):
        k0 = pl.multiple_of(c * tk, tk)
        pv = jnp.dot(vt_ref[:, pl.ds(k0, tk)], p_ref[slot], preferred_element_type=F32)
        acc_ref[...] = alpha_ref[slot] * acc_ref[...] + pv

    n_chunks = seq // tk
    scores(0, 0)
    scores(1, 1)
    softmax(0, 0)

    def pair(i, carry):
        c = 2 * i
        scores(c + 2, 0)
        values(c, 0)
        softmax(c + 1, 1)
        scores(c + 3, 1)
        values(c + 1, 1)
        softmax(c + 2, 0)
        return carry

    lax.fori_loop(0, n_chunks // 2 - 1, pair, 0, unroll=unroll)
    values(n_chunks - 2, 0)
    softmax(n_chunks - 1, 1)
    values(n_chunks - 1, 1)
    ot = acc_ref[:dv, :] / acc_ref[dv:dv + 1, :]
    if diff:
        lam = (jnp.exp(jnp.sum(lq1_ref[...] * lk1_ref[...], keepdims=True))
               - jnp.exp(jnp.sum(lq2_ref[...] * lk2_ref[...], keepdims=True)) + lambda_init)
        od = ot[:, :tq] - lam * ot[:, tq:]
        r = lax.rsqrt(jnp.mean(od * od, axis=0, keepdims=True) + EPS)
        od = od * r * (subg_ref[...] * (1.0 - lambda_init))
        o_ref[0] = od.T.astype(BF16)
    else:
        pad = jnp.zeros((LANES - dv, tq), F32)
        for g in range(groups):
            og = jnp.concatenate([ot[:, g * tq:(g + 1) * tq], pad], axis=0)
            o_ref[0, :, LANES * g:LANES * (g + 1)] = og.T.astype(BF16)


def _attention(q, k, vt, groups, tiles, diff_args=None, lambda_init=0.0):
    b, s, _ = q.shape
    nkv = k.shape[2] // LANES
    tq, tk, unroll = tiles
    tq, tk = min(tq, s), min(tk, s)
    n_pairs = s // (2 * tk) - 1
    assert s % (2 * tk) == 0 and n_pairs >= 0, (s, tk)
    unroll = max(1, min(unroll, n_pairs))
    cols = groups * tq
    vrows = vt.shape[0] // nkv
    scratch = [pltpu.VMEM((cols, LANES), BF16),
               pltpu.VMEM((2, tk, cols), F32),
               pltpu.VMEM((2, tk, cols), BF16),
               pltpu.VMEM((2, 1, cols), F32),
               pltpu.VMEM((1, cols), F32),
               pltpu.VMEM((vrows, cols), F32)]
    diff = diff_args is not None
    q_spec = pl.BlockSpec((1, tq, groups * LANES), lambda bi, h, i: (bi, i, h))
    k_spec = pl.BlockSpec((1, s, LANES), lambda bi, h, i: (bi, 0, h))
    vt_spec = pl.BlockSpec((vrows, s), lambda bi, h, i: (h, bi))
    in_specs = [q_spec, k_spec, vt_spec]
    args = [q, k, vt]
    out_w = 1 if diff else groups
    if diff:
        slopes, lq1, lk1, lq2, lk2, subg = diff_args
        in_specs = ([pl.BlockSpec(memory_space=pltpu.SMEM)] + in_specs
                    + [_const_spec((1, C_HEAD_DIM))] * 4 + [_const_spec((LANES, 1))])
        args = [slopes] + args + [lq1, lk1, lq2, lk2, subg]
    return pl.pallas_call(
        functools.partial(_attn_body, groups=groups, tq=tq, tk=tk, unroll=unroll, seq=s,
                          diff=diff, lambda_init=lambda_init),
        grid=(b, nkv, s // tq),
        in_specs=in_specs,
        out_specs=pl.BlockSpec((1, tq, out_w * LANES), lambda bi, h, i: (bi, i, h)),
        out_shape=jax.ShapeDtypeStruct((b, s, nkv * out_w * LANES), BF16),
        scratch_shapes=scratch,
        compiler_params=_cparams(3),
        name="attn_diff" if diff else f"attn_g{groups}",
    )(*args)


def _pad_slabs(w, n, width, offset=0):
    kdim = w.shape[0]
    w = w.reshape(kdim, n, width)
    w = jnp.pad(w, ((0, 0), (0, 0), (offset, LANES - width - offset)))
    return w.reshape(kdim, n * LANES)


def _rot_cols(w, n, perm, sign):
    kdim = w.shape[0]
    width = perm.shape[0]
    w = w.reshape(kdim, n, width)[:, :, perm] * sign
    return w.reshape(kdim, n * width)


_HALF = B_ROPE // 2
_PERM32 = np.concatenate([np.arange(_HALF, 2 * _HALF), np.arange(_HALF)])
_SIGN32 = np.concatenate([-np.ones(_HALF), np.ones(_HALF)]).astype(np.float32)
_PERM64 = np.concatenate([_PERM32, _PERM32 + 2 * _HALF])
_SIGN64 = np.concatenate([_SIGN32, _SIGN32])


def _rope_tables(seq):
    t = jnp.arange(seq, dtype=jnp.int32)
    inv = ROPE_BASE ** (-jnp.arange(_HALF, dtype=F32) * (2.0 / (2 * _HALF)))

    def cs(pos):
        ang = pos.astype(F32)[:, None] * inv[None, :]
        return jnp.cos(ang), jnp.sin(ang)

    cr, sr = cs(t // GRID_W)
    cc, sc = cs(t % GRID_W)
    ct, st = cs(t)
    za = jnp.zeros((seq, LANES - A_HEAD_DIM), F32)
    cosa = jnp.concatenate([cr, cr, cc, cc, za], axis=1)
    sina = jnp.concatenate([sr, sr, sc, sc, za], axis=1)
    zb = jnp.zeros((seq, LANES - B_NOPE - B_ROPE), F32)
    cosb = jnp.concatenate([jnp.ones((seq, B_NOPE), F32), ct, ct, zb], axis=1)
    sinb = jnp.concatenate([jnp.zeros((seq, B_NOPE), F32), st, st, zb], axis=1)
    return cosa, sina, cosb, sinb


def _vt_weights(w, dv):
    kdim, n, _ = w.shape
    w = jnp.pad(w, ((0, 0), (0, 0), (0, ONES_PAD)))
    return w.reshape(kdim, n * (dv + ONES_PAD)).T.astype(BF16)


def _lane_row(g, perm=None):
    g = g.astype(F32)
    if perm is not None:
        g = g[perm]
    return jnp.pad(g, (0, LANES - g.shape[0])).reshape(1, LANES)


def _prep_even(w_in, a_qn, a_kn, b_qn, b_wuq, b_kvn, b_wukv, w_out):
    na, nk = A_HEADS * A_HEAD_DIM, A_KV_HEADS * A_HEAD_DIM
    wq, wk, wv = w_in[:, :na], w_in[:, na:na + nk], w_in[:, na + nk:na + 2 * nk]
    c = na + 2 * nk
    wcq, wckv = w_in[:, c:c + B_Q_RANK], w_in[:, c + B_Q_RANK:c + B_Q_RANK + B_KV_RANK]
    wkr = w_in[:, c + B_Q_RANK + B_KV_RANK:]
    w1 = jnp.concatenate([
        _pad_slabs(wq, A_HEADS, A_HEAD_DIM),
        _pad_slabs(_rot_cols(wq, A_HEADS, _PERM64, _SIGN64), A_HEADS, A_HEAD_DIM),
        _pad_slabs(wk, A_KV_HEADS, A_HEAD_DIM),
        _pad_slabs(_rot_cols(wk, A_KV_HEADS, _PERM64, _SIGN64), A_KV_HEADS, A_HEAD_DIM),
        wcq, wckv,
        _pad_slabs(wkr, 1, B_ROPE, offset=B_NOPE),
        _pad_slabs(_rot_cols(wkr, 1, _PERM32, _SIGN32), 1, B_ROPE, offset=B_NOPE),
    ], axis=1).astype(BF16)
    dq = B_NOPE + B_ROPE
    uq = b_wuq.reshape(B_Q_RANK, B_HEADS, dq)
    uq_rot = uq[:, :, B_NOPE:][:, :, _PERM32] * _SIGN32
    wuq = jnp.concatenate([
        _pad_slabs(uq.reshape(B_Q_RANK, B_HEADS * dq), B_HEADS, dq),
        _pad_slabs(uq_rot.reshape(B_Q_RANK, B_HEADS * B_ROPE), B_HEADS, B_ROPE, offset=B_NOPE),
    ], axis=1).astype(BF16)
    ukv = b_wukv.reshape(B_KV_RANK, B_HEADS, B_NOPE + B_VDIM)
    wukv = _pad_slabs(ukv[:, :, :B_NOPE].reshape(B_KV_RANK, -1), B_HEADS, B_NOPE).astype(BF16)
    wvbt = _vt_weights(ukv[:, :, B_NOPE:], B_VDIM)
    wvat = _vt_weights(wv.reshape(D_MODEL, A_KV_HEADS, A_HEAD_DIM), A_HEAD_DIM)
    wo_a = jnp.pad(w_out[:na].reshape(A_HEADS, A_HEAD_DIM, D_MODEL),
                   ((0, 0), (0, LANES - A_HEAD_DIM), (0, 0))).reshape(A_HEADS * LANES, D_MODEL)
    wo_b = jnp.pad(w_out[na:].reshape(B_HEADS, B_VDIM, D_MODEL),
                   ((0, 0), (0, LANES - B_VDIM), (0, 0))).reshape(B_HEADS * LANES, D_MODEL)
    return dict(
        w1=w1, wvat=wvat, wuq=wuq, wukv=wukv, wvbt=wvbt, wo_a=wo_a.astype(BF16), wo_b=wo_b.astype(BF16),
        gq=_lane_row(a_qn), gqp=_lane_row(a_qn, _PERM64),
        gk=_lane_row(a_kn), gkp=_lane_row(a_kn, _PERM64),
        bqn=b_qn.reshape(1, -1), bkvn=b_kvn.reshape(1, -1))


def _prep_odd(w_in, w_out):
    n = C_HEADS * 2 * C_HEAD_DIM
    wq = w_in[:, :n].reshape(D_MODEL, C_HEADS, 2, C_HEAD_DIM)
    zeros = jnp.zeros_like(wq[:, :, 0])
    q1 = jnp.concatenate([wq[:, :, 0], zeros], axis=-1)
    q2 = jnp.concatenate([zeros, wq[:, :, 1]], axis=-1)
    wq2 = jnp.stack([q1, q2], axis=2).reshape(D_MODEL, 2 * n)
    w = jnp.concatenate([wq2, w_in[:, n:2 * n]], axis=1).astype(BF16)
    wvt = _vt_weights(w_in[:, 2 * n:].reshape(D_MODEL, C_HEADS, 2 * C_HEAD_DIM), 2 * C_HEAD_DIM)
    return dict(w=w, wvt=wvt, wo=w_out.astype(BF16))


def _alibi_slopes():
    return 2.0 ** (-8.0 * jnp.arange(1, C_HEADS + 1, dtype=F32) / C_HEADS)


def _trunk(x, p, lw, final_norm, tm_ffn=512, tm_proj=256,
           tiles_a=(128, 256, 2), tiles_b=(512, 256, 2), tiles_c=(256, 256, 2)):
    b, s, d = x.shape
    t = b * s
    tm_ffn, tm_proj = min(tm_ffn, t), min(tm_proj, s)
    x = x.reshape(t, d)
    tabs = _rope_tables(s)
    slopes = _alibi_slopes()
    fg = final_norm.reshape(1, d)
    for i in range(DEPTH):
        w = lw[i]
        x = _ffn(x, w["ffn1_norm"], w["ffn1_wg"], w["ffn1_wu"], w["ffn1_wd"], tm_ffn)
        if i % 2 == 0:
            m = w["mix"]
            qa, ka, vat, qb, kb, vbt = _inproj_even(
                x, w["mix_norm"], m["w1"], m["wvat"], tabs, m["gq"], m["gqp"], m["gk"], m["gkp"],
                m["bqn"], m["wuq"], m["bkvn"], m["wukv"], m["wvbt"], s, tm_proj)
            r3 = lambda a: a.reshape(b, s, a.shape[1])
            oa = _attention(r3(qa), r3(ka), vat, A_HEADS // A_KV_HEADS, tiles_a)
            ob = _attention(r3(qb), r3(kb), vbt, 1, tiles_b)
            x = _outproj(x, [(oa.reshape(t, -1), m["wo_a"]), (ob.reshape(t, -1), m["wo_b"])], tm_ffn)
        else:
            m = w["mix"]
            lambda_init = 0.8 - 0.6 * math.exp(-0.3 * i)
            q, k, vt = _inproj_odd(x, w["mix_norm"], m["w"], m["wvt"], tm_proj)
            r3 = lambda a: a.reshape(b, s, a.shape[1])
            oc = _attention(r3(q), r3(k), vt, 2, tiles_c,
                            diff_args=(slopes, w["lq1"], w["lk1"], w["lq2"], w["lk2"], w["subg"]),
                            lambda_init=lambda_init)
            x = _outproj(x, [(oc.reshape(t, -1), m["wo"])], tm_ffn)
        x = _ffn(x, w["ffn2_norm"], w["ffn2_wg"], w["ffn2_wu"], w["ffn2_wd"], tm_ffn)
        x = _ple(x, p[i].reshape(t, -1), w["ple_norm"], w["ple_w_gate"], w["ple_w_proj"], fg,
                 i == DEPTH - 1, tm_ffn)
    return x.reshape(b, s, d)


def _layer_weights(ffn1_norm, ffn1_wg, ffn1_wu, ffn1_wd, mix_norm, ab_w_in, a_q_norm, a_k_norm,
                   b_q_norm, b_w_uq, b_kv_norm, b_w_ukv, ab_w_out, c_w_in, c_lambda_q1,
                   c_lambda_k1, c_lambda_q2, c_lambda_k2, c_sub_norm, c_w_out, ffn2_norm, ffn2_wg,
                   ffn2_wu, ffn2_wd, ple_norm, ple_w_gate, ple_w_proj):
    layers = []
    for i in range(DEPTH):
        w = dict(
            ffn1_norm=ffn1_norm[i].reshape(1, -1), ffn1_wg=ffn1_wg[i].astype(BF16),
            ffn1_wu=ffn1_wu[i].astype(BF16), ffn1_wd=ffn1_wd[i].astype(BF16),
            mix_norm=mix_norm[i].reshape(1, -1),
            ffn2_norm=ffn2_norm[i].reshape(1, -1), ffn2_wg=ffn2_wg[i].astype(BF16),
            ffn2_wu=ffn2_wu[i].astype(BF16), ffn2_wd=ffn2_wd[i].astype(BF16),
            ple_norm=ple_norm[i].reshape(1, -1), ple_w_gate=ple_w_gate[i].astype(BF16),
            ple_w_proj=ple_w_proj[i].astype(BF16))
        j = i // 2
        if i % 2 == 0:
            w["mix"] = _prep_even(ab_w_in[j], a_q_norm[j], a_k_norm[j], b_q_norm[j], b_w_uq[j],
                                  b_kv_norm[j], b_w_ukv[j], ab_w_out[j])
        else:
            w["mix"] = _prep_odd(c_w_in[j], c_w_out[j])
            w["lq1"] = c_lambda_q1[j].reshape(1, -1)
            w["lk1"] = c_lambda_k1[j].reshape(1, -1)
            w["lq2"] = c_lambda_q2[j].reshape(1, -1)
            w["lk2"] = c_lambda_k2[j].reshape(1, -1)
            w["subg"] = c_sub_norm[j].reshape(-1, 1)
        layers.append(w)
    return layers


def kernel(x_prompt, x_sample, p_prompt, p_sample, ffn1_norm, ffn1_wg, ffn1_wu, ffn1_wd, mix_norm, ab_w_in, a_q_norm, a_k_norm, b_q_norm, b_w_uq, b_kv_norm, b_w_ukv, ab_w_out, c_w_in, c_lambda_q1, c_lambda_k1, c_lambda_q2, c_lambda_k2, c_sub_norm, c_w_out, ffn2_norm, ffn2_wg, ffn2_wu, ffn2_wd, ple_norm, ple_w_gate, ple_w_proj, final_norm):
    lw = _layer_weights(ffn1_norm, ffn1_wg, ffn1_wu, ffn1_wd, mix_norm, ab_w_in, a_q_norm,
                        a_k_norm, b_q_norm, b_w_uq, b_kv_norm, b_w_ukv, ab_w_out, c_w_in,
                        c_lambda_q1, c_lambda_k1, c_lambda_q2, c_lambda_k2, c_sub_norm, c_w_out,
                        ffn2_norm, ffn2_wg, ffn2_wu, ffn2_wd, ple_norm, ple_w_gate, ple_w_proj)
    y_prompt = _trunk(x_prompt, p_prompt, lw, final_norm)
    y_sample = _trunk(x_sample, p_sample, lw, final_norm)
    return (y_prompt, y_sample)
```

```python
import functools
import math

import numpy as np
import jax
import jax.numpy as jnp
from jax import lax
from jax.experimental import pallas as pl
from jax.experimental.pallas import tpu as pltpu

D_MODEL = 1024
DEPTH = 4
GRID_W = 64
EPS = 1e-6
D_FF = 2816
D_PLE = 256
ROPE_BASE = 10000.0
A_HEADS = 8
A_KV_HEADS = 2
A_HEAD_DIM = 64
B_HEADS = 8
B_NOPE = 64
B_ROPE = 32
B_VDIM = 64
B_Q_RANK = 256
B_KV_RANK = 128
C_HEADS = 8
C_HEAD_DIM = 64

LANES = 128
VMEM_LIMIT = 56 * 1024 * 1024
NEG_BIG = -1e30
LOG2E = math.log2(math.e)
ONES_PAD = 16

F32 = jnp.float32
BF16 = jnp.bfloat16


def _cparams(n_axes):
    return pltpu.CompilerParams(
        dimension_semantics=("parallel",) * n_axes, vmem_limit_bytes=VMEM_LIMIT)


def _const_spec(shape):
    nd = len(shape)
    return pl.BlockSpec(shape, lambda *_: (0,) * nd, pipeline_mode=pl.Buffered(1))


def _rms(x, g):
    return x * lax.rsqrt(jnp.mean(x * x, axis=-1, keepdims=True) + EPS) * g


def _sigmoid(x):
    return 1.0 / (1.0 + jnp.exp(-x))


def _ffn_body(x_ref, g_ref, wg_ref, wu_ref, wd_ref, o_ref):
    x = x_ref[...]
    xn = _rms(x, g_ref[...]).astype(BF16)
    g = jnp.dot(xn, wg_ref[...], preferred_element_type=F32)
    u = jnp.dot(xn, wu_ref[...], preferred_element_type=F32)
    a = (g * _sigmoid(g) * u).astype(BF16)
    y = jnp.dot(a, wd_ref[...], preferred_element_type=F32)
    o_ref[...] = x + 0.5 * y


def _ffn(x, g, wg, wu, wd, tm):
    t, d = x.shape
    f = wg.shape[1]
    return pl.pallas_call(
        _ffn_body,
        grid=(t // tm,),
        in_specs=[
            pl.BlockSpec((tm, d), lambda i: (i, 0)),
            _const_spec((1, d)),
            _const_spec((d, f)),
            _const_spec((d, f)),
            _const_spec((f, d)),
        ],
        out_specs=pl.BlockSpec((tm, d), lambda i: (i, 0)),
        out_shape=jax.ShapeDtypeStruct((t, d), F32),
        compiler_params=_cparams(1),
        name="ffn",
    )(x, g, wg, wu, wd)


def _slab(ref_or_val, i):
    return ref_or_val[:, LANES * i:LANES * (i + 1)]


def _dot_nt(a, b):
    return lax.dot_general(a, b, (((1,), (1,)), ((), ())), preferred_element_type=F32)


def _values_t(wvt, h, dv):
    vt = _dot_nt(wvt, h)
    row = lax.broadcasted_iota(jnp.int32, vt.shape, 0)
    return jnp.where(row % (dv + ONES_PAD) == dv, 1.0, vt).astype(BF16)


def _inproj_even_body(x_ref, g_ref, w1_ref, wvat_ref, cosa_ref, sina_ref, cosb_ref, sinb_ref,
                      gq_ref, gqp_ref, gk_ref, gkp_ref, bqn_ref, wuq_ref, bkvn_ref, wukv_ref,
                      wvbt_ref, qa_ref, ka_ref, vat_ref, qb_ref, kb_ref, vbt_ref):
    h = _rms(x_ref[...], g_ref[...]).astype(BF16)
    z = jnp.dot(h, w1_ref[...], preferred_element_type=F32)
    vat_ref[...] = _values_t(wvat_ref[...], h, A_HEAD_DIM)
    cosa, sina = cosa_ref[...], sina_ref[...]
    cosb, sinb = cosb_ref[...], sinb_ref[...]

    scale_a = A_HEAD_DIM ** -0.5 * LOG2E
    cq, sq = cosa * (gq_ref[...] * scale_a), sina * (gqp_ref[...] * scale_a)
    ck, sk = cosa * gk_ref[...], sina * gkp_ref[...]
    o_main, o_rot = 0, A_HEADS
    for i in range(A_HEADS):
        zz, zr = _slab(z, o_main + i), _slab(z, o_rot + i)
        r = lax.rsqrt(jnp.sum(zz * zz, axis=-1, keepdims=True) * (1.0 / A_HEAD_DIM) + EPS)
        qa_ref[:, LANES * i:LANES * (i + 1)] = ((zz * cq + zr * sq) * r).astype(BF16)
    o_main, o_rot = 2 * A_HEADS, 2 * A_HEADS + A_KV_HEADS
    for i in range(A_KV_HEADS):
        zz, zr = _slab(z, o_main + i), _slab(z, o_rot + i)
        r = lax.rsqrt(jnp.sum(zz * zz, axis=-1, keepdims=True) * (1.0 / A_HEAD_DIM) + EPS)
        ka_ref[:, LANES * i:LANES * (i + 1)] = ((zz * ck + zr * sk) * r).astype(BF16)

    c0 = LANES * (2 * A_HEADS + 2 * A_KV_HEADS)
    scale_b = (B_NOPE + B_ROPE) ** -0.5 * LOG2E
    cqn = _rms(z[:, c0:c0 + B_Q_RANK], bqn_ref[...]).astype(BF16)
    zq = jnp.dot(cqn, wuq_ref[...], preferred_element_type=F32)
    for i in range(B_HEADS):
        qb_ref[:, LANES * i:LANES * (i + 1)] = (
            (_slab(zq, i) * cosb + _slab(zq, B_HEADS + i) * sinb) * scale_b).astype(BF16)
    c1 = c0 + B_Q_RANK
    ckvn = _rms(z[:, c1:c1 + B_KV_RANK], bkvn_ref[...]).astype(BF16)
    zk = jnp.dot(ckvn, wukv_ref[...], preferred_element_type=F32)
    vbt_ref[...] = _values_t(wvbt_ref[...], ckvn, B_VDIM)
    c2 = c1 + B_KV_RANK
    kr = z[:, c2:c2 + LANES] * cosb + z[:, c2 + LANES:c2 + 2 * LANES] * sinb
    for i in range(B_HEADS):
        kb_ref[:, LANES * i:LANES * (i + 1)] = (_slab(zk, i) + kr).astype(BF16)


def _inproj_even(x, g, w1, wvat, tabs, gq, gqp, gk, gkp, bqn, wuq, bkvn, wukv, wvbt, seq, tm):
    t, d = x.shape
    nt = seq // tm
    tab_spec = pl.BlockSpec((tm, LANES), lambda i: (i % nt, 0))
    row = lambda i: (i, 0)
    col = lambda i: (0, i)
    outs = ((A_HEADS * LANES, row), (A_KV_HEADS * LANES, row), (wvat.shape[0], col),
            (B_HEADS * LANES, row), (B_HEADS * LANES, row), (wvbt.shape[0], col))
    return pl.pallas_call(
        _inproj_even_body,
        grid=(t // tm,),
        in_specs=[pl.BlockSpec((tm, d), row), _const_spec((1, d)), _const_spec(w1.shape),
                  _const_spec(wvat.shape),
                  tab_spec, tab_spec, tab_spec, tab_spec,
                  _const_spec((1, LANES)), _const_spec((1, LANES)),
                  _const_spec((1, LANES)), _const_spec((1, LANES)),
                  _const_spec((1, B_Q_RANK)), _const_spec(wuq.shape),
                  _const_spec((1, B_KV_RANK)), _const_spec(wukv.shape),
                  _const_spec(wvbt.shape)],
        out_specs=[pl.BlockSpec((tm, n) if m is row else (n, tm), m) for n, m in outs],
        out_shape=[jax.ShapeDtypeStruct((t, n) if m is row else (n, t), BF16)
                   for n, m in outs],
        compiler_params=_cparams(1),
        name="inproj_even",
    )(x, g, w1, wvat, *tabs, gq, gqp, gk, gkp, bqn, wuq, bkvn, wukv, wvbt)


def _inproj_odd_body(x_ref, g_ref, w_ref, wvt_ref, q_ref, k_ref, vt_ref):
    h = _rms(x_ref[...], g_ref[...]).astype(BF16)
    z = jnp.dot(h, w_ref[...], preferred_element_type=F32)
    nq = 2 * C_HEADS * LANES
    q_ref[...] = (z[:, :nq] * (C_HEAD_DIM ** -0.5 * LOG2E)).astype(BF16)
    k_ref[...] = z[:, nq:].astype(BF16)
    vt_ref[...] = _values_t(wvt_ref[...], h, 2 * C_HEAD_DIM)


def _inproj_odd(x, g, w, wvt, tm):
    t, d = x.shape
    row = lambda i: (i, 0)
    return pl.pallas_call(
        _inproj_odd_body,
        grid=(t // tm,),
        in_specs=[pl.BlockSpec((tm, d), row), _const_spec((1, d)), _const_spec(w.shape),
                  _const_spec(wvt.shape)],
        out_specs=[pl.BlockSpec((tm, 2 * C_HEADS * LANES), row),
                   pl.BlockSpec((tm, C_HEADS * LANES), row),
                   pl.BlockSpec((wvt.shape[0], tm), lambda i: (0, i))],
        out_shape=[jax.ShapeDtypeStruct((t, 2 * C_HEADS * LANES), BF16),
                   jax.ShapeDtypeStruct((t, C_HEADS * LANES), BF16),
                   jax.ShapeDtypeStruct((wvt.shape[0], t), BF16)],
        compiler_params=_cparams(1),
        name="inproj_odd",
    )(x, g, w, wvt)


def _outproj_body(*refs):
    x_ref, o_ref = refs[0], refs[-1]
    acc = x_ref[...]
    pairs = refs[1:-1]
    for a_ref, w_ref in zip(pairs[0::2], pairs[1::2]):
        acc = acc + jnp.dot(a_ref[...], w_ref[...], preferred_element_type=F32)
    o_ref[...] = acc


def _outproj(x, pairs, tm):
    t, d = x.shape
    row = lambda i: (i, 0)
    in_specs = [pl.BlockSpec((tm, d), row)]
    args = [x]
    for a, w in pairs:
        in_specs += [pl.BlockSpec((tm, a.shape[1]), row), _const_spec(w.shape)]
        args += [a, w]
    return pl.pallas_call(
        _outproj_body,
        grid=(t // tm,),
        in_specs=in_specs,
        out_specs=pl.BlockSpec((tm, d), row),
        out_shape=jax.ShapeDtypeStruct((t, d), F32),
        compiler_params=_cparams(1),
        name="outproj",
    )(*args)


def _ple_body(x_ref, p_ref, g_ref, wgate_ref, wproj_ref, fg_ref, o_ref, *, final):
    x = x_ref[...]
    xn = _rms(x, g_ref[...]).astype(BF16)
    gate = _sigmoid(jnp.dot(xn, wgate_ref[...], preferred_element_type=F32))
    proj = jnp.dot(p_ref[...].astype(BF16), wproj_ref[...], preferred_element_type=F32)
    y = x + gate * proj
    if final:
        y = _rms(y, fg_ref[...])
    o_ref[...] = y


def _ple(x, p, g, wgate, wproj, fg, final, tm):
    t, d = x.shape
    row = lambda i: (i, 0)
    return pl.pallas_call(
        functools.partial(_ple_body, final=final),
        grid=(t // tm,),
        in_specs=[pl.BlockSpec((tm, d), row), pl.BlockSpec((tm, p.shape[1]), row),
                  _const_spec((1, d)), _const_spec(wgate.shape), _const_spec(wproj.shape),
                  _const_spec((1, d))],
        out_specs=pl.BlockSpec((tm, d), row),
        out_shape=jax.ShapeDtypeStruct((t, d), F32),
        compiler_params=_cparams(1),
        name="ple",
    )(x, p, g, wgate, wproj, fg)


def _attn_body(*refs, groups, tq, tk, unroll, seq, diff, lambda_init):
    if diff:
        (slopes_ref, q_ref, k_ref, vt_ref, lq1_ref, lk1_ref, lq2_ref, lk2_ref, subg_ref, o_ref,
         qs_ref, s_ref, p_ref, alpha_ref, m_ref, acc_ref, bias_ref) = refs
    else:
        q_ref, k_ref, vt_ref, o_ref, qs_ref, s_ref, p_ref, alpha_ref, m_ref, acc_ref = refs
    dv = vt_ref.shape[0] - ONES_PAD
    for g in range(groups):
        qs_ref[g * tq:(g + 1) * tq, :] = q_ref[0, :, LANES * g:LANES * (g + 1)]
    m_ref[...] = jnp.full(m_ref.shape, NEG_BIG, F32)
    acc_ref[...] = jnp.zeros(acc_ref.shape, F32)
    if diff:
        slope = slopes_ref[pl.program_id(1)] * LOG2E
        q0 = pl.program_id(2) * tq
        rel = (lax.broadcasted_iota(jnp.int32, (tk, tq), 1)
               - lax.broadcasted_iota(jnp.int32, (tk, tq), 0)).astype(F32)
        bias_ref[0] = slope * rel
        bias_ref[1] = -slope * jnp.abs(rel)
        bias_ref[2] = -slope * rel

    def scores(c, slot):
        k0 = pl.multiple_of(c * tk, tk)
        s_ref[slot] = _dot_nt(k_ref[0, pl.ds(k0, tk), :], qs_ref[...])

    def softmax(c, slot):
        s = s_ref[slot]
        m = m_ref[...]
        if diff:
            delta = q0 - c * tk
            tile = bias_ref[jnp.sign(delta) + 1]
            s = s + jnp.concatenate([tile] * groups, axis=1)
            off = slope * jnp.abs(delta).astype(F32)
            m_new = jnp.maximum(m, jnp.max(s, axis=0, keepdims=True) - off)
            shift = m_new + off
        else:
            m_new = jnp.maximum(m, jnp.max(s, axis=0, keepdims=True))
            shift = m_new
        m_ref[...] = m_new
        alpha_ref[slot] = jnp.exp2(m - m_new)
        p_ref[slot] = jnp.exp2(s - shift).astype(BF16)

    def values(c, slot):
        k0 = pl.multiple_of(c * tk, tk)
        pv = jnp.dot(vt_ref[:, pl.ds(k0, tk)], p_ref[slot], preferred_element_type=F32)
        acc_ref[...] = alpha_ref[slot] * acc_ref[...] + pv

    n_chunks = seq // tk
    scores(0, 0)
    scores(1, 1)
    softmax(0, 0)

    def pair(i, carry):
        c = 2 * i
        scores(c + 2, 0)
        values(c, 0)
        softmax(c + 1, 1)
        scores(c + 3, 1)
        values(c + 1, 1)
        softmax(c + 2, 0)
        return carry

    lax.fori_loop(0, n_chunks // 2 - 1, pair, 0, unroll=unroll)
    values(n_chunks - 2, 0)
    softmax(n_chunks - 1, 1)
    values(n_chunks - 1, 1)
    ot = acc_ref[:dv, :] / acc_ref[dv:dv + 1, :]
    if diff:
        lam = (jnp.exp(jnp.sum(lq1_ref[...] * lk1_ref[...], keepdims=True))
               - jnp.exp(jnp.sum(lq2_ref[...] * lk2_ref[...], keepdims=True)) + lambda_init)
        od = ot[:, :tq] - lam * ot[:, tq:]
        r = lax.rsqrt(jnp.mean(od * od, axis=0, keepdims=True) + EPS)
        od = od * r * (subg_ref[...] * (1.0 - lambda_init))
        o_ref[0] = od.T.astype(BF16)
    else:
        pad = jnp.zeros((LANES - dv, tq), F32)
        for g in range(groups):
            og = jnp.concatenate([ot[:, g * tq:(g + 1) * tq], pad], axis=0)
            o_ref[0, :, LANES * g:LANES * (g + 1)] = og.T.astype(BF16)


def _attention(q, k, vt, groups, tiles, diff_args=None, lambda_init=0.0):
    b, s, _ = q.shape
    nkv = k.shape[2] // LANES
    tq, tk, unroll = tiles
    tq, tk = min(tq, s), min(tk, s)
    n_pairs = s // (2 * tk) - 1
    assert s % (2 * tk) == 0 and n_pairs >= 0, (s, tk)
    unroll = max(1, min(unroll, n_pairs))
    cols = groups * tq
    vrows = vt.shape[0] // nkv
    scratch = [pltpu.VMEM((cols, LANES), BF16),
               pltpu.VMEM((2, tk, cols), F32),
               pltpu.VMEM((2, tk, cols), BF16),
               pltpu.VMEM((2, 1, cols), F32),
               pltpu.VMEM((1, cols), F32),
               pltpu.VMEM((vrows, cols), F32)]
    diff = diff_args is not None
    q_spec = pl.BlockSpec((1, tq, groups * LANES), lambda bi, h, i: (bi, i, h))
    k_spec = pl.BlockSpec((1, s, LANES), lambda bi, h, i: (bi, 0, h))
    vt_spec = pl.BlockSpec((vrows, s), lambda bi, h, i: (h, bi))
    in_specs = [q_spec, k_spec, vt_spec]
    args = [q, k, vt]
    out_w = 1 if diff else groups
    if diff:
        assert tq == tk, (tq, tk)
        scratch.append(pltpu.VMEM((3, tk, tq), F32))
        slopes, lq1, lk1, lq2, lk2, subg = diff_args
        in_specs = ([pl.BlockSpec(memory_space=pltpu.SMEM)] + in_specs
                    + [_const_spec((1, C_HEAD_DIM))] * 4 + [_const_spec((LANES, 1))])
        args = [slopes] + args + [lq1, lk1, lq2, lk2, subg]
    return pl.pallas_call(
        functools.partial(_attn_body, groups=groups, tq=tq, tk=tk, unroll=unroll, seq=s,
                          diff=diff, lambda_init=lambda_init),
        grid=(b, nkv, s // tq),
        in_specs=in_specs,
        out_specs=pl.BlockSpec((1, tq, out_w * LANES), lambda bi, h, i: (bi, i, h)),
        out_shape=jax.ShapeDtypeStruct((b, s, nkv * out_w * LANES), BF16),
        scratch_shapes=scratch,
        compiler_params=_cparams(3),
        name="attn_diff" if diff else f"attn_g{groups}",
    )(*args)


def _pad_slabs(w, n, width, offset=0):
    kdim = w.shape[0]
    w = w.reshape(kdim, n, width)
    w = jnp.pad(w, ((0, 0), (0, 0), (offset, LANES - width - offset)))
    return w.reshape(kdim, n * LANES)


def _rot_cols(w, n, perm, sign):
    kdim = w.shape[0]
    width = perm.shape[0]
    w = w.reshape(kdim, n, width)[:, :, perm] * sign
    return w.reshape(kdim, n * width)


_HALF = B_ROPE // 2
_PERM32 = np.concatenate([np.arange(_HALF, 2 * _HALF), np.arange(_HALF)])
_SIGN32 = np.concatenate([-np.ones(_HALF), np.ones(_HALF)]).astype(np.float32)
_PERM64 = np.concatenate([_PERM32, _PERM32 + 2 * _HALF])
_SIGN64 = np.concatenate([_SIGN32, _SIGN32])


def _rope_tables(seq):
    t = jnp.arange(seq, dtype=jnp.int32)
    inv = ROPE_BASE ** (-jnp.arange(_HALF, dtype=F32) * (2.0 / (2 * _HALF)))

    def cs(pos):
        ang = pos.astype(F32)[:, None] * inv[None, :]
        return jnp.cos(ang), jnp.sin(ang)

    cr, sr = cs(t // GRID_W)
    cc, sc = cs(t % GRID_W)
    ct, st = cs(t)
    za = jnp.zeros((seq, LANES - A_HEAD_DIM), F32)
    cosa = jnp.concatenate([cr, cr, cc, cc, za], axis=1)
    sina = jnp.concatenate([sr, sr, sc, sc, za], axis=1)
    zb = jnp.zeros((seq, LANES - B_NOPE - B_ROPE), F32)
    cosb = jnp.concatenate([jnp.ones((seq, B_NOPE), F32), ct, ct, zb], axis=1)
    sinb = jnp.concatenate([jnp.zeros((seq, B_NOPE), F32), st, st, zb], axis=1)
    return cosa, sina, cosb, sinb


def _vt_weights(w, dv):
    kdim, n, _ = w.shape
    w = jnp.pad(w, ((0, 0), (0, 0), (0, ONES_PAD)))
    return w.reshape(kdim, n * (dv + ONES_PAD)).T.astype(BF16)


def _lane_row(g, perm=None):
    g = g.astype(F32)
    if perm is not None:
        g = g[perm]
    return jnp.pad(g, (0, LANES - g.shape[0])).reshape(1, LANES)


def _prep_even(w_in, a_qn, a_kn, b_qn, b_wuq, b_kvn, b_wukv, w_out):
    na, nk = A_HEADS * A_HEAD_DIM, A_KV_HEADS * A_HEAD_DIM
    wq, wk, wv = w_in[:, :na], w_in[:, na:na + nk], w_in[:, na + nk:na + 2 * nk]
    c = na + 2 * nk
    wcq, wckv = w_in[:, c:c + B_Q_RANK], w_in[:, c + B_Q_RANK:c + B_Q_RANK + B_KV_RANK]
    wkr = w_in[:, c + B_Q_RANK + B_KV_RANK:]
    w1 = jnp.concatenate([
        _pad_slabs(wq, A_HEADS, A_HEAD_DIM),
        _pad_slabs(_rot_cols(wq, A_HEADS, _PERM64, _SIGN64), A_HEADS, A_HEAD_DIM),
        _pad_slabs(wk, A_KV_HEADS, A_HEAD_DIM),
        _pad_slabs(_rot_cols(wk, A_KV_HEADS, _PERM64, _SIGN64), A_KV_HEADS, A_HEAD_DIM),
        wcq, wckv,
        _pad_slabs(wkr, 1, B_ROPE, offset=B_NOPE),
        _pad_slabs(_rot_cols(wkr, 1, _PERM32, _SIGN32), 1, B_ROPE, offset=B_NOPE),
    ], axis=1).astype(BF16)
    dq = B_NOPE + B_ROPE
    uq = b_wuq.reshape(B_Q_RANK, B_HEADS, dq)
    uq_rot = uq[:, :, B_NOPE:][:, :, _PERM32] * _SIGN32
    wuq = jnp.concatenate([
        _pad_slabs(uq.reshape(B_Q_RANK, B_HEADS * dq), B_HEADS, dq),
        _pad_slabs(uq_rot.reshape(B_Q_RANK, B_HEADS * B_ROPE), B_HEADS, B_ROPE, offset=B_NOPE),
    ], axis=1).astype(BF16)
    ukv = b_wukv.reshape(B_KV_RANK, B_HEADS, B_NOPE + B_VDIM)
    wukv = _pad_slabs(ukv[:, :, :B_NOPE].reshape(B_KV_RANK, -1), B_HEADS, B_NOPE).astype(BF16)
    wvbt = _vt_weights(ukv[:, :, B_NOPE:], B_VDIM)
    wvat = _vt_weights(wv.reshape(D_MODEL, A_KV_HEADS, A_HEAD_DIM), A_HEAD_DIM)
    wo_a = jnp.pad(w_out[:na].reshape(A_HEADS, A_HEAD_DIM, D_MODEL),
                   ((0, 0), (0, LANES - A_HEAD_DIM), (0, 0))).reshape(A_HEADS * LANES, D_MODEL)
    wo_b = jnp.pad(w_out[na:].reshape(B_HEADS, B_VDIM, D_MODEL),
                   ((0, 0), (0, LANES - B_VDIM), (0, 0))).reshape(B_HEADS * LANES, D_MODEL)
    return dict(
        w1=w1, wvat=wvat, wuq=wuq, wukv=wukv, wvbt=wvbt, wo_a=wo_a.astype(BF16), wo_b=wo_b.astype(BF16),
        gq=_lane_row(a_qn), gqp=_lane_row(a_qn, _PERM64),
        gk=_lane_row(a_kn), gkp=_lane_row(a_kn, _PERM64),
        bqn=b_qn.reshape(1, -1), bkvn=b_kvn.reshape(1, -1))


def _prep_odd(w_in, w_out):
    n = C_HEADS * 2 * C_HEAD_DIM
    wq = w_in[:, :n].reshape(D_MODEL, C_HEADS, 2, C_HEAD_DIM)
    zeros = jnp.zeros_like(wq[:, :, 0])
    q1 = jnp.concatenate([wq[:, :, 0], zeros], axis=-1)
    q2 = jnp.concatenate([zeros, wq[:, :, 1]], axis=-1)
    wq2 = jnp.stack([q1, q2], axis=2).reshape(D_MODEL, 2 * n)
    w = jnp.concatenate([wq2, w_in[:, n:2 * n]], axis=1).astype(BF16)
    wvt = _vt_weights(w_in[:, 2 * n:].reshape(D_MODEL, C_HEADS, 2 * C_HEAD_DIM), 2 * C_HEAD_DIM)
    return dict(w=w, wvt=wvt, wo=w_out.astype(BF16))


def _alibi_slopes():
    return 2.0 ** (-8.0 * jnp.arange(1, C_HEADS + 1, dtype=F32) / C_HEADS)


def _trunk(x, p, lw, final_norm, tm_ffn=512, tm_proj=256,
           tiles_a=(128, 256, 15), tiles_b=(512, 256, 15), tiles_c=(256, 256, 15)):
    b, s, d = x.shape
    t = b * s
    tm_ffn, tm_proj = min(tm_ffn, t), min(tm_proj, s)
    x = x.reshape(t, d)
    tabs = _rope_tables(s)
    slopes = _alibi_slopes()
    fg = final_norm.reshape(1, d)
    for i in range(DEPTH):
        w = lw[i]
        x = _ffn(x, w["ffn1_norm"], w["ffn1_wg"], w["ffn1_wu"], w["ffn1_wd"], tm_ffn)
        if i % 2 == 0:
            m = w["mix"]
            qa, ka, vat, qb, kb, vbt = _inproj_even(
                x, w["mix_norm"], m["w1"], m["wvat"], tabs, m["gq"], m["gqp"], m["gk"], m["gkp"],
                m["bqn"], m["wuq"], m["bkvn"], m["wukv"], m["wvbt"], s, tm_proj)
            r3 = lambda a: a.reshape(b, s, a.shape[1])
            oa = _attention(r3(qa), r3(ka), vat, A_HEADS // A_KV_HEADS, tiles_a)
            ob = _attention(r3(qb), r3(kb), vbt, 1, tiles_b)
            x = _outproj(x, [(oa.reshape(t, -1), m["wo_a"]), (ob.reshape(t, -1), m["wo_b"])], tm_ffn)
        else:
            m = w["mix"]
            lambda_init = 0.8 - 0.6 * math.exp(-0.3 * i)
            q, k, vt = _inproj_odd(x, w["mix_norm"], m["w"], m["wvt"], tm_proj)
            r3 = lambda a: a.reshape(b, s, a.shape[1])
            oc = _attention(r3(q), r3(k), vt, 2, tiles_c,
                            diff_args=(slopes, w["lq1"], w["lk1"], w["lq2"], w["lk2"], w["subg"]),
                            lambda_init=lambda_init)
            x = _outproj(x, [(oc.reshape(t, -1), m["wo"])], tm_ffn)
        x = _ffn(x, w["ffn2_norm"], w["ffn2_wg"], w["ffn2_wu"], w["ffn2_wd"], tm_ffn)
        x = _ple(x, p[i].reshape(t, -1), w["ple_norm"], w["ple_w_gate"], w["ple_w_proj"], fg,
                 i == DEPTH - 1, tm_ffn)
    return x.reshape(b, s, d)


def _layer_weights(ffn1_norm, ffn1_wg, ffn1_wu, ffn1_wd, mix_norm, ab_w_in, a_q_norm, a_k_norm,
                   b_q_norm, b_w_uq, b_kv_norm, b_w_ukv, ab_w_out, c_w_in, c_lambda_q1,
                   c_lambda_k1, c_lambda_q2, c_lambda_k2, c_sub_norm, c_w_out, ffn2_norm, ffn2_wg,
                   ffn2_wu, ffn2_wd, ple_norm, ple_w_gate, ple_w_proj):
    layers = []
    for i in range(DEPTH):
        w = dict(
            ffn1_norm=ffn1_norm[i].reshape(1, -1), ffn1_wg=ffn1_wg[i].astype(BF16),
            ffn1_wu=ffn1_wu[i].astype(BF16), ffn1_wd=ffn1_wd[i].astype(BF16),
            mix_norm=mix_norm[i].reshape(1, -1),
            ffn2_norm=ffn2_norm[i].reshape(1, -1), ffn2_wg=ffn2_wg[i].astype(BF16),
            ffn2_wu=ffn2_wu[i].astype(BF16), ffn2_wd=ffn2_wd[i].astype(BF16),
            ple_norm=ple_norm[i].reshape(1, -1), ple_w_gate=ple_w_gate[i].astype(BF16),
            ple_w_proj=ple_w_proj[i].astype(BF16))
        j = i // 2
        if i % 2 == 0:
            w["mix"] = _prep_even(ab_w_in[j], a_q_norm[j], a_k_norm[j], b_q_norm[j], b_w_uq[j],
                                  b_kv_norm[j], b_w_ukv[j], ab_w_out[j])
        else:
            w["mix"] = _prep_odd(c_w_in[j], c_w_out[j])
            w["lq1"] = c_lambda_q1[j].reshape(1, -1)
            w["lk1"] = c_lambda_k1[j].reshape(1, -1)
            w["lq2"] = c_lambda_q2[j].reshape(1, -1)
            w["lk2"] = c_lambda_k2[j].reshape(1, -1)
            w["subg"] = c_sub_norm[j].reshape(-1, 1)
        layers.append(w)
    return layers


def kernel(x_prompt, x_sample, p_prompt, p_sample, ffn1_norm, ffn1_wg, ffn1_wu, ffn1_wd, mix_norm, ab_w_in, a_q_norm, a_k_norm, b_q_norm, b_w_uq, b_kv_norm, b_w_ukv, ab_w_out, c_w_in, c_lambda_q1, c_lambda_k1, c_lambda_q2, c_lambda_k2, c_sub_norm, c_w_out, ffn2_norm, ffn2_wg, ffn2_wu, ffn2_wd, ple_norm, ple_w_gate, ple_w_proj, final_norm):
    lw = _layer_weights(ffn1_norm, ffn1_wg, ffn1_wu, ffn1_wd, mix_norm, ab_w_in, a_q_norm,
                        a_k_norm, b_q_norm, b_w_uq, b_kv_norm, b_w_ukv, ab_w_out, c_w_in,
                        c_lambda_q1, c_lambda_k1, c_lambda_q2, c_lambda_k2, c_sub_norm, c_w_out,
                        ffn2_norm, ffn2_wg, ffn2_wu, ffn2_wd, ple_norm, ple_w_gate, ple_w_proj)
    y_prompt = _trunk(x_prompt, p_prompt, lw, final_norm)
    y_sample = _trunk(x_sample, p_sample, lw, final_norm)
    return (y_prompt, y_sample)
```

```python
import functools
import math

import numpy as np
import jax
import jax.numpy as jnp
from jax import lax
from jax.experimental import pallas as pl
from jax.experimental.pallas import tpu as pltpu

D_MODEL = 1024
DEPTH = 4
GRID_W = 64
EPS = 1e-6
D_FF = 2816
D_PLE = 256
ROPE_BASE = 10000.0
A_HEADS = 8
A_KV_HEADS = 2
A_HEAD_DIM = 64
B_HEADS = 8
B_NOPE = 64
B_ROPE = 32
B_VDIM = 64
B_Q_RANK = 256
B_KV_RANK = 128
C_HEADS = 8
C_HEAD_DIM = 64

LANES = 128
VMEM_LIMIT = 56 * 1024 * 1024
NEG_BIG = -1e30
LOG2E = math.log2(math.e)
ONES_PAD = 16

F32 = jnp.float32
BF16 = jnp.bfloat16


def _cparams(n_axes):
    return pltpu.CompilerParams(
        dimension_semantics=("parallel",) * n_axes, vmem_limit_bytes=VMEM_LIMIT)


def _const_spec(shape):
    nd = len(shape)
    return pl.BlockSpec(shape, lambda *_: (0,) * nd, pipeline_mode=pl.Buffered(1))


def _rms(x, g):
    return x * lax.rsqrt(jnp.mean(x * x, axis=-1, keepdims=True) + EPS) * g


def _sigmoid(x):
    return 1.0 / (1.0 + jnp.exp(-x))


def _ffn_body(x_ref, g_ref, wg_ref, wu_ref, wd_ref, o_ref):
    x = x_ref[...]
    xn = _rms(x, g_ref[...]).astype(BF16)
    g = jnp.dot(xn, wg_ref[...], preferred_element_type=F32)
    u = jnp.dot(xn, wu_ref[...], preferred_element_type=F32)
    a = (g * _sigmoid(g) * u).astype(BF16)
    y = jnp.dot(a, wd_ref[...], preferred_element_type=F32)
    o_ref[...] = x + 0.5 * y


def _ffn(x, g, wg, wu, wd, tm):
    t, d = x.shape
    f = wg.shape[1]
    return pl.pallas_call(
        _ffn_body,
        grid=(t // tm,),
        in_specs=[
            pl.BlockSpec((tm, d), lambda i: (i, 0)),
            _const_spec((1, d)),
            _const_spec((d, f)),
            _const_spec((d, f)),
            _const_spec((f, d)),
        ],
        out_specs=pl.BlockSpec((tm, d), lambda i: (i, 0)),
        out_shape=jax.ShapeDtypeStruct((t, d), F32),
        compiler_params=_cparams(1),
        name="ffn",
    )(x, g, wg, wu, wd)


def _slab(ref_or_val, i):
    return ref_or_val[:, LANES * i:LANES * (i + 1)]


def _dot_nt(a, b):
    return lax.dot_general(a, b, (((1,), (1,)), ((), ())), preferred_element_type=F32)


def _values_t(wvt, h, dv):
    vt = _dot_nt(wvt, h)
    row = lax.broadcasted_iota(jnp.int32, vt.shape, 0)
    return jnp.where(row % (dv + ONES_PAD) == dv, 1.0, vt).astype(BF16)


def _inproj_even_body(x_ref, g_ref, w1_ref, wvat_ref, cosa_ref, sina_ref, cosb_ref, sinb_ref,
                      gq_ref, gqp_ref, gk_ref, gkp_ref, bqn_ref, wuq_ref, bkvn_ref, wukv_ref,
                      wvbt_ref, qa_ref, ka_ref, vat_ref, qb_ref, kb_ref, vbt_ref):
    h = _rms(x_ref[...], g_ref[...]).astype(BF16)
    z = jnp.dot(h, w1_ref[...], preferred_element_type=F32)
    vat_ref[...] = _values_t(wvat_ref[...], h, A_HEAD_DIM)
    cosa, sina = cosa_ref[...], sina_ref[...]
    cosb, sinb = cosb_ref[...], sinb_ref[...]

    scale_a = A_HEAD_DIM ** -0.5 * LOG2E
    cq, sq = cosa * (gq_ref[...] * scale_a), sina * (gqp_ref[...] * scale_a)
    ck, sk = cosa * gk_ref[...], sina * gkp_ref[...]
    o_main, o_rot = 0, A_HEADS
    for i in range(A_HEADS):
        zz, zr = _slab(z, o_main + i), _slab(z, o_rot + i)
        r = lax.rsqrt(jnp.sum(zz * zz, axis=-1, keepdims=True) * (1.0 / A_HEAD_DIM) + EPS)
        qa_ref[:, LANES * i:LANES * (i + 1)] = ((zz * cq + zr * sq) * r).astype(BF16)
    o_main, o_rot = 2 * A_HEADS, 2 * A_HEADS + A_KV_HEADS
    for i in range(A_KV_HEADS):
        zz, zr = _slab(z, o_main + i), _slab(z, o_rot + i)
        r = lax.rsqrt(jnp.sum(zz * zz, axis=-1, keepdims=True) * (1.0 / A_HEAD_DIM) + EPS)
        ka_ref[:, LANES * i:LANES * (i + 1)] = ((zz * ck + zr * sk) * r).astype(BF16)

    c0 = LANES * (2 * A_HEADS + 2 * A_KV_HEADS)
    scale_b = (B_NOPE + B_ROPE) ** -0.5 * LOG2E
    cqn = _rms(z[:, c0:c0 + B_Q_RANK], bqn_ref[...]).astype(BF16)
    zq = jnp.dot(cqn, wuq_ref[...], preferred_element_type=F32)
    for i in range(B_HEADS):
        qb_ref[:, LANES * i:LANES * (i + 1)] = (
            (_slab(zq, i) * cosb + _slab(zq, B_HEADS + i) * sinb) * scale_b).astype(BF16)
    c1 = c0 + B_Q_RANK
    ckvn = _rms(z[:, c1:c1 + B_KV_RANK], bkvn_ref[...]).astype(BF16)
    zk = jnp.dot(ckvn, wukv_ref[...], preferred_element_type=F32)
    vbt_ref[...] = _values_t(wvbt_ref[...], ckvn, B_VDIM)
    c2 = c1 + B_KV_RANK
    kr = z[:, c2:c2 + LANES] * cosb + z[:, c2 + LANES:c2 + 2 * LANES] * sinb
    for i in range(B_HEADS):
        kb_ref[:, LANES * i:LANES * (i + 1)] = (_slab(zk, i) + kr).astype(BF16)


def _inproj_even(x, g, w1, wvat, tabs, gq, gqp, gk, gkp, bqn, wuq, bkvn, wukv, wvbt, seq, tm):
    t, d = x.shape
    nt = seq // tm
    tab_spec = pl.BlockSpec((tm, LANES), lambda i: (i % nt, 0))
    row = lambda i: (i, 0)
    col = lambda i: (0, i)
    outs = ((A_HEADS * LANES, row), (A_KV_HEADS * LANES, row), (wvat.shape[0], col),
            (B_HEADS * LANES, row), (B_HEADS * LANES, row), (wvbt.shape[0], col))
    return pl.pallas_call(
        _inproj_even_body,
        grid=(t // tm,),
        in_specs=[pl.BlockSpec((tm, d), row), _const_spec((1, d)), _const_spec(w1.shape),
                  _const_spec(wvat.shape),
                  tab_spec, tab_spec, tab_spec, tab_spec,
                  _const_spec((1, LANES)), _const_spec((1, LANES)),
                  _const_spec((1, LANES)), _const_spec((1, LANES)),
                  _const_spec((1, B_Q_RANK)), _const_spec(wuq.shape),
                  _const_spec((1, B_KV_RANK)), _const_spec(wukv.shape),
                  _const_spec(wvbt.shape)],
        out_specs=[pl.BlockSpec((tm, n) if m is row else (n, tm), m) for n, m in outs],
        out_shape=[jax.ShapeDtypeStruct((t, n) if m is row else (n, t), BF16)
                   for n, m in outs],
        compiler_params=_cparams(1),
        name="inproj_even",
    )(x, g, w1, wvat, *tabs, gq, gqp, gk, gkp, bqn, wuq, bkvn, wukv, wvbt)


def _inproj_odd_body(x_ref, g_ref, w_ref, wvt_ref, q_ref, k_ref, vt_ref):
    h = _rms(x_ref[...], g_ref[...]).astype(BF16)
    z = jnp.dot(h, w_ref[...], preferred_element_type=F32)
    nq = 2 * C_HEADS * LANES
    q_ref[...] = (z[:, :nq] * (C_HEAD_DIM ** -0.5 * LOG2E)).astype(BF16)
    k_ref[...] = z[:, nq:].astype(BF16)
    vt_ref[...] = _values_t(wvt_ref[...], h, 2 * C_HEAD_DIM)


def _inproj_odd(x, g, w, wvt, tm):
    t, d = x.shape
    row = lambda i: (i, 0)
    return pl.pallas_call(
        _inproj_odd_body,
        grid=(t // tm,),
        in_specs=[pl.BlockSpec((tm, d), row), _const_spec((1, d)), _const_spec(w.shape),
                  _const_spec(wvt.shape)],
        out_specs=[pl.BlockSpec((tm, 2 * C_HEADS * LANES), row),
                   pl.BlockSpec((tm, C_HEADS * LANES), row),
                   pl.BlockSpec((wvt.shape[0], tm), lambda i: (0, i))],
        out_shape=[jax.ShapeDtypeStruct((t, 2 * C_HEADS * LANES), BF16),
                   jax.ShapeDtypeStruct((t, C_HEADS * LANES), BF16),
                   jax.ShapeDtypeStruct((wvt.shape[0], t), BF16)],
        compiler_params=_cparams(1),
        name="inproj_odd",
    )(x, g, w, wvt)


def _outproj_body(*refs):
    x_ref, o_ref = refs[0], refs[-1]
    acc = x_ref[...]
    pairs = refs[1:-1]
    for a_ref, w_ref in zip(pairs[0::2], pairs[1::2]):
        acc = acc + jnp.dot(a_ref[...], w_ref[...], preferred_element_type=F32)
    o_ref[...] = acc


def _outproj(x, pairs, tm):
    t, d = x.shape
    row = lambda i: (i, 0)
    in_specs = [pl.BlockSpec((tm, d), row)]
    args = [x]
    for a, w in pairs:
        in_specs += [pl.BlockSpec((tm, a.shape[1]), row), _const_spec(w.shape)]
        args += [a, w]
    return pl.pallas_call(
        _outproj_body,
        grid=(t // tm,),
        in_specs=in_specs,
        out_specs=pl.BlockSpec((tm, d), row),
        out_shape=jax.ShapeDtypeStruct((t, d), F32),
        compiler_params=_cparams(1),
        name="outproj",
    )(*args)


def _ple_body(x_ref, p_ref, g_ref, wgate_ref, wproj_ref, fg_ref, o_ref, *, final):
    x = x_ref[...]
    xn = _rms(x, g_ref[...]).astype(BF16)
    gate = _sigmoid(jnp.dot(xn, wgate_ref[...], preferred_element_type=F32))
    proj = jnp.dot(p_ref[...].astype(BF16), wproj_ref[...], preferred_element_type=F32)
    y = x + gate * proj
    if final:
        y = _rms(y, fg_ref[...])
    o_ref[...] = y


def _ple(x, p, g, wgate, wproj, fg, final, tm):
    t, d = x.shape
    row = lambda i: (i, 0)
    return pl.pallas_call(
        functools.partial(_ple_body, final=final),
        grid=(t // tm,),
        in_specs=[pl.BlockSpec((tm, d), row), pl.BlockSpec((tm, p.shape[1]), row),
                  _const_spec((1, d)), _const_spec(wgate.shape), _const_spec(wproj.shape),
                  _const_spec((1, d))],
        out_specs=pl.BlockSpec((tm, d), row),
        out_shape=jax.ShapeDtypeStruct((t, d), F32),
        compiler_params=_cparams(1),
        name="ple",
    )(x, p, g, wgate, wproj, fg)


def _attn_body(*refs, groups, tq, tk, nt, seq, diff, lambda_init):
    if diff:
        (slopes_ref, q_ref, k_ref, vt_ref, lq1_ref, lk1_ref, lq2_ref, lk2_ref, subg_ref, o_ref,
         qs_ref, s_ref, p_ref, alpha_ref, m_ref, acc_ref, bias_ref) = refs
    else:
        q_ref, k_ref, vt_ref, o_ref, qs_ref, s_ref, p_ref, alpha_ref, m_ref, acc_ref = refs
    dv = vt_ref.shape[0] - ONES_PAD
    for t in range(nt):
        for g in range(groups):
            qs_ref[t, g * tq:(g + 1) * tq, :] = (
                q_ref[0, t * tq:(t + 1) * tq, LANES * g:LANES * (g + 1)])
    m_ref[...] = jnp.full(m_ref.shape, NEG_BIG, F32)
    acc_ref[...] = jnp.zeros(acc_ref.shape, F32)
    if diff:
        slope = slopes_ref[pl.program_id(1)] * LOG2E
        q_base = pl.program_id(2) * (nt * tq)
        rel = (lax.broadcasted_iota(jnp.int32, (tk, tq), 1)
               - lax.broadcasted_iota(jnp.int32, (tk, tq), 0)).astype(F32)
        bias_ref[0] = slope * rel
        bias_ref[1] = -slope * jnp.abs(rel)
        bias_ref[2] = -slope * rel

    def scores(item, slot):
        t, c = item
        s_ref[slot] = _dot_nt(k_ref[0, c * tk:(c + 1) * tk, :], qs_ref[t])

    def softmax(item, slot):
        t, c = item
        s = s_ref[slot]
        m = m_ref[t]
        if diff:
            delta = q_base + (t * tq - c * tk)
            tile = bias_ref[jnp.sign(delta) + 1]
            s = s + jnp.concatenate([tile] * groups, axis=1)
            off = slope * jnp.abs(delta).astype(F32)
            m_new = jnp.maximum(m, jnp.max(s, axis=0, keepdims=True) - off)
            shift = m_new + off
        else:
            m_new = jnp.maximum(m, jnp.max(s, axis=0, keepdims=True))
            shift = m_new
        m_ref[t] = m_new
        alpha_ref[slot] = jnp.exp2(m - m_new)
        p_ref[slot] = jnp.exp2(s - shift).astype(BF16)

    def values(item, slot):
        t, c = item
        pv = jnp.dot(vt_ref[:, c * tk:(c + 1) * tk], p_ref[slot], preferred_element_type=F32)
        acc_ref[t] = alpha_ref[slot] * acc_ref[t] + pv

    def finalize(t):
        ot = acc_ref[t, :dv, :] / acc_ref[t, dv:dv + 1, :]
        rows = slice(t * tq, (t + 1) * tq)
        if diff:
            lam = (jnp.exp(jnp.sum(lq1_ref[...] * lk1_ref[...], keepdims=True))
                   - jnp.exp(jnp.sum(lq2_ref[...] * lk2_ref[...], keepdims=True)) + lambda_init)
            od = ot[:, :tq] - lam * ot[:, tq:]
            r = lax.rsqrt(jnp.mean(od * od, axis=0, keepdims=True) + EPS)
            od = od * r * (subg_ref[...] * (1.0 - lambda_init))
            o_ref[0, rows, :] = od.T.astype(BF16)
        else:
            pad = jnp.zeros((LANES - dv, tq), F32)
            for g in range(groups):
                og = jnp.concatenate([ot[:, g * tq:(g + 1) * tq], pad], axis=0)
                o_ref[0, rows, LANES * g:LANES * (g + 1)] = og.T.astype(BF16)

    n_chunks = seq // tk
    items = [(t, c) for t in range(nt) for c in range(n_chunks)]
    scores(items[0], 0)
    scores(items[1], 1)
    softmax(items[0], 0)
    for u, item in enumerate(items):
        if u + 2 < len(items):
            scores(items[u + 2], u % 2)
        values(item, u % 2)
        if item[1] == n_chunks - 1:
            finalize(item[0])
        if u + 1 < len(items):
            softmax(items[u + 1], (u + 1) % 2)


def _attention(q, k, vt, groups, tiles, diff_args=None, lambda_init=0.0):
    b, s, _ = q.shape
    nkv = k.shape[2] // LANES
    tq, tk, items = tiles
    tq, tk = min(tq, s), min(tk, s)
    nt = max(1, min(s // tq, items // (s // tk)))
    assert s % tk == 0 and s // tk >= 2 and s % (nt * tq) == 0, (s, tiles)
    cols = groups * tq
    vrows = vt.shape[0] // nkv
    scratch = [pltpu.VMEM((nt, cols, LANES), BF16),
               pltpu.VMEM((2, tk, cols), F32),
               pltpu.VMEM((2, tk, cols), BF16),
               pltpu.VMEM((2, 1, cols), F32),
               pltpu.VMEM((nt, 1, cols), F32),
               pltpu.VMEM((nt, vrows, cols), F32)]
    diff = diff_args is not None
    q_spec = pl.BlockSpec((1, nt * tq, groups * LANES), lambda bi, h, i: (bi, i, h))
    k_spec = pl.BlockSpec((1, s, LANES), lambda bi, h, i: (bi, 0, h))
    vt_spec = pl.BlockSpec((vrows, s), lambda bi, h, i: (h, bi))
    in_specs = [q_spec, k_spec, vt_spec]
    args = [q, k, vt]
    out_w = 1 if diff else groups
    if diff:
        assert tq == tk, (tq, tk)
        scratch.append(pltpu.VMEM((3, tk, tq), F32))
        slopes, lq1, lk1, lq2, lk2, subg = diff_args
        in_specs = ([pl.BlockSpec(memory_space=pltpu.SMEM)] + in_specs
                    + [_const_spec((1, C_HEAD_DIM))] * 4 + [_const_spec((LANES, 1))])
        args = [slopes] + args + [lq1, lk1, lq2, lk2, subg]
    return pl.pallas_call(
        functools.partial(_attn_body, groups=groups, tq=tq, tk=tk, nt=nt, seq=s,
                          diff=diff, lambda_init=lambda_init),
        grid=(b, nkv, s // (nt * tq)),
        in_specs=in_specs,
        out_specs=pl.BlockSpec((1, nt * tq, out_w * LANES), lambda bi, h, i: (bi, i, h)),
        out_shape=jax.ShapeDtypeStruct((b, s, nkv * out_w * LANES), BF16),
        scratch_shapes=scratch,
        compiler_params=_cparams(3),
        name="attn_diff" if diff else f"attn_g{groups}",
    )(*args)


def _pad_slabs(w, n, width, offset=0):
    kdim = w.shape[0]
    w = w.reshape(kdim, n, width)
    w = jnp.pad(w, ((0, 0), (0, 0), (offset, LANES - width - offset)))
    return w.reshape(kdim, n * LANES)


def _rot_cols(w, n, perm, sign):
    kdim = w.shape[0]
    width = perm.shape[0]
    w = w.reshape(kdim, n, width)[:, :, perm] * sign
    return w.reshape(kdim, n * width)


_HALF = B_ROPE // 2
_PERM32 = np.concatenate([np.arange(_HALF, 2 * _HALF), np.arange(_HALF)])
_SIGN32 = np.concatenate([-np.ones(_HALF), np.ones(_HALF)]).astype(np.float32)
_PERM64 = np.concatenate([_PERM32, _PERM32 + 2 * _HALF])
_SIGN64 = np.concatenate([_SIGN32, _SIGN32])


def _rope_tables(seq):
    t = jnp.arange(seq, dtype=jnp.int32)
    inv = ROPE_BASE ** (-jnp.arange(_HALF, dtype=F32) * (2.0 / (2 * _HALF)))

    def cs(pos):
        ang = pos.astype(F32)[:, None] * inv[None, :]
        return jnp.cos(ang), jnp.sin(ang)

    cr, sr = cs(t // GRID_W)
    cc, sc = cs(t % GRID_W)
    ct, st = cs(t)
    za = jnp.zeros((seq, LANES - A_HEAD_DIM), F32)
    cosa = jnp.concatenate([cr, cr, cc, cc, za], axis=1)
    sina = jnp.concatenate([sr, sr, sc, sc, za], axis=1)
    zb = jnp.zeros((seq, LANES - B_NOPE - B_ROPE), F32)
    cosb = jnp.concatenate([jnp.ones((seq, B_NOPE), F32), ct, ct, zb], axis=1)
    sinb = jnp.concatenate([jnp.zeros((seq, B_NOPE), F32), st, st, zb], axis=1)
    return cosa, sina, cosb, sinb


def _vt_weights(w, dv):
    kdim, n, _ = w.shape
    w = jnp.pad(w, ((0, 0), (0, 0), (0, ONES_PAD)))
    return w.reshape(kdim, n * (dv + ONES_PAD)).T.astype(BF16)


def _lane_row(g, perm=None):
    g = g.astype(F32)
    if perm is not None:
        g = g[perm]
    return jnp.pad(g, (0, LANES - g.shape[0])).reshape(1, LANES)


def _prep_even(w_in, a_qn, a_kn, b_qn, b_wuq, b_kvn, b_wukv, w_out):
    na, nk = A_HEADS * A_HEAD_DIM, A_KV_HEADS * A_HEAD_DIM
    wq, wk, wv = w_in[:, :na], w_in[:, na:na + nk], w_in[:, na + nk:na + 2 * nk]
    c = na + 2 * nk
    wcq, wckv = w_in[:, c:c + B_Q_RANK], w_in[:, c + B_Q_RANK:c + B_Q_RANK + B_KV_RANK]
    wkr = w_in[:, c + B_Q_RANK + B_KV_RANK:]
    w1 = jnp.concatenate([
        _pad_slabs(wq, A_HEADS, A_HEAD_DIM),
        _pad_slabs(_rot_cols(wq, A_HEADS, _PERM64, _SIGN64), A_HEADS, A_HEAD_DIM),
        _pad_slabs(wk, A_KV_HEADS, A_HEAD_DIM),
        _pad_slabs(_rot_cols(wk, A_KV_HEADS, _PERM64, _SIGN64), A_KV_HEADS, A_HEAD_DIM),
        wcq, wckv,
        _pad_slabs(wkr, 1, B_ROPE, offset=B_NOPE),
        _pad_slabs(_rot_cols(wkr, 1, _PERM32, _SIGN32), 1, B_ROPE, offset=B_NOPE),
    ], axis=1).astype(BF16)
    dq = B_NOPE + B_ROPE
    uq = b_wuq.reshape(B_Q_RANK, B_HEADS, dq)
    uq_rot = uq[:, :, B_NOPE:][:, :, _PERM32] * _SIGN32
    wuq = jnp.concatenate([
        _pad_slabs(uq.reshape(B_Q_RANK, B_HEADS * dq), B_HEADS, dq),
        _pad_slabs(uq_rot.reshape(B_Q_RANK, B_HEADS * B_ROPE), B_HEADS, B_ROPE, offset=B_NOPE),
    ], axis=1).astype(BF16)
    ukv = b_wukv.reshape(B_KV_RANK, B_HEADS, B_NOPE + B_VDIM)
    wukv = _pad_slabs(ukv[:, :, :B_NOPE].reshape(B_KV_RANK, -1), B_HEADS, B_NOPE).astype(BF16)
    wvbt = _vt_weights(ukv[:, :, B_NOPE:], B_VDIM)
    wvat = _vt_weights(wv.reshape(D_MODEL, A_KV_HEADS, A_HEAD_DIM), A_HEAD_DIM)
    wo_a = jnp.pad(w_out[:na].reshape(A_HEADS, A_HEAD_DIM, D_MODEL),
                   ((0, 0), (0, LANES - A_HEAD_DIM), (0, 0))).reshape(A_HEADS * LANES, D_MODEL)
    wo_b = jnp.pad(w_out[na:].reshape(B_HEADS, B_VDIM, D_MODEL),
                   ((0, 0), (0, LANES - B_VDIM), (0, 0))).reshape(B_HEADS * LANES, D_MODEL)
    return dict(
        w1=w1, wvat=wvat, wuq=wuq, wukv=wukv, wvbt=wvbt, wo_a=wo_a.astype(BF16), wo_b=wo_b.astype(BF16),
        gq=_lane_row(a_qn), gqp=_lane_row(a_qn, _PERM64),
        gk=_lane_row(a_kn), gkp=_lane_row(a_kn, _PERM64),
        bqn=b_qn.reshape(1, -1), bkvn=b_kvn.reshape(1, -1))


def _prep_odd(w_in, w_out):
    n = C_HEADS * 2 * C_HEAD_DIM
    wq = w_in[:, :n].reshape(D_MODEL, C_HEADS, 2, C_HEAD_DIM)
    zeros = jnp.zeros_like(wq[:, :, 0])
    q1 = jnp.concatenate([wq[:, :, 0], zeros], axis=-1)
    q2 = jnp.concatenate([zeros, wq[:, :, 1]], axis=-1)
    wq2 = jnp.stack([q1, q2], axis=2).reshape(D_MODEL, 2 * n)
    w = jnp.concatenate([wq2, w_in[:, n:2 * n]], axis=1).astype(BF16)
    wvt = _vt_weights(w_in[:, 2 * n:].reshape(D_MODEL, C_HEADS, 2 * C_HEAD_DIM), 2 * C_HEAD_DIM)
    return dict(w=w, wvt=wvt, wo=w_out.astype(BF16))


def _alibi_slopes():
    return 2.0 ** (-8.0 * jnp.arange(1, C_HEADS + 1, dtype=F32) / C_HEADS)


def _trunk(x, p, lw, final_norm, tm_ffn=512, tm_proj=256,
           tiles_a=(128, 256, 64), tiles_b=(512, 256, 64), tiles_c=(256, 256, 64)):
    b, s, d = x.shape
    t = b * s
    tm_ffn, tm_proj = min(tm_ffn, t), min(tm_proj, s)
    x = x.reshape(t, d)
    tabs = _rope_tables(s)
    slopes = _alibi_slopes()
    fg = final_norm.reshape(1, d)
    for i in range(DEPTH):
        w = lw[i]
        x = _ffn(x, w["ffn1_norm"], w["ffn1_wg"], w["ffn1_wu"], w["ffn1_wd"], tm_ffn)
        if i % 2 == 0:
            m = w["mix"]
            qa, ka, vat, qb, kb, vbt = _inproj_even(
                x, w["mix_norm"], m["w1"], m["wvat"], tabs, m["gq"], m["gqp"], m["gk"], m["gkp"],
                m["bqn"], m["wuq"], m["bkvn"], m["wukv"], m["wvbt"], s, tm_proj)
            r3 = lambda a: a.reshape(b, s, a.shape[1])
            oa = _attention(r3(qa), r3(ka), vat, A_HEADS // A_KV_HEADS, tiles_a)
            ob = _attention(r3(qb), r3(kb), vbt, 1, tiles_b)
            x = _outproj(x, [(oa.reshape(t, -1), m["wo_a"]), (ob.reshape(t, -1), m["wo_b"])], tm_ffn)
        else:
            m = w["mix"]
            lambda_init = 0.8 - 0.6 * math.exp(-0.3 * i)
            q, k, vt = _inproj_odd(x, w["mix_norm"], m["w"], m["wvt"], tm_proj)
            r3 = lambda a: a.reshape(b, s, a.shape[1])
            oc = _attention(r3(q), r3(k), vt, 2, tiles_c,
                            diff_args=(slopes, w["lq1"], w["lk1"], w["lq2"], w["lk2"], w["subg"]),
                            lambda_init=lambda_init)
            x = _outproj(x, [(oc.reshape(t, -1), m["wo"])], tm_ffn)
        x = _ffn(x, w["ffn2_norm"], w["ffn2_wg"], w["ffn2_wu"], w["ffn2_wd"], tm_ffn)
        x = _ple(x, p[i].reshape(t, -1), w["ple_norm"], w["ple_w_gate"], w["ple_w_proj"], fg,
                 i == DEPTH - 1, tm_ffn)
    return x.reshape(b, s, d)


def _layer_weights(ffn1_norm, ffn1_wg, ffn1_wu, ffn1_wd, mix_norm, ab_w_in, a_q_norm, a_k_norm,
                   b_q_norm, b_w_uq, b_kv_norm, b_w_ukv, ab_w_out, c_w_in, c_lambda_q1,
                   c_lambda_k1, c_lambda_q2, c_lambda_k2, c_sub_norm, c_w_out, ffn2_norm, ffn2_wg,
                   ffn2_wu, ffn2_wd, ple_norm, ple_w_gate, ple_w_proj):
    layers = []
    for i in range(DEPTH):
        w = dict(
            ffn1_norm=ffn1_norm[i].reshape(1, -1), ffn1_wg=ffn1_wg[i].astype(BF16),
            ffn1_wu=ffn1_wu[i].astype(BF16), ffn1_wd=ffn1_wd[i].astype(BF16),
            mix_norm=mix_norm[i].reshape(1, -1),
            ffn2_norm=ffn2_norm[i].reshape(1, -1), ffn2_wg=ffn2_wg[i].astype(BF16),
            ffn2_wu=ffn2_wu[i].astype(BF16), ffn2_wd=ffn2_wd[i].astype(BF16),
            ple_norm=ple_norm[i].reshape(1, -1), ple_w_gate=ple_w_gate[i].astype(BF16),
            ple_w_proj=ple_w_proj[i].astype(BF16))
        j = i // 2
        if i % 2 == 0:
            w["mix"] = _prep_even(ab_w_in[j], a_q_norm[j], a_k_norm[j], b_q_norm[j], b_w_uq[j],
                                  b_kv_norm[j], b_w_ukv[j], ab_w_out[j])
        else:
            w["mix"] = _prep_odd(c_w_in[j], c_w_out[j])
            w["lq1"] = c_lambda_q1[j].reshape(1, -1)
            w["lk1"] = c_lambda_k1[j].reshape(1, -1)
            w["lq2"] = c_lambda_q2[j].reshape(1, -1)
            w["lk2"] = c_lambda_k2[j].reshape(1, -1)
            w["subg"] = c_sub_norm[j].reshape(-1, 1)
        layers.append(w)
    return layers


def kernel(x_prompt, x_sample, p_prompt, p_sample, ffn1_norm, ffn1_wg, ffn1_wu, ffn1_wd, mix_norm, ab_w_in, a_q_norm, a_k_norm, b_q_norm, b_w_uq, b_kv_norm, b_w_ukv, ab_w_out, c_w_in, c_lambda_q1, c_lambda_k1, c_lambda_q2, c_lambda_k2, c_sub_norm, c_w_out, ffn2_norm, ffn2_wg, ffn2_wu, ffn2_wd, ple_norm, ple_w_gate, ple_w_proj, final_norm):
    lw = _layer_weights(ffn1_norm, ffn1_wg, ffn1_wu, ffn1_wd, mix_norm, ab_w_in, a_q_norm,
                        a_k_norm, b_q_norm, b_w_uq, b_kv_norm, b_w_ukv, ab_w_out, c_w_in,
                        c_lambda_q1, c_lambda_k1, c_lambda_q2, c_lambda_k2, c_sub_norm, c_w_out,
                        ffn2_norm, ffn2_wg, ffn2_wu, ffn2_wd, ple_norm, ple_w_gate, ple_w_proj)
    y_prompt = _trunk(x_prompt, p_prompt, lw, final_norm)
    y_sample = _trunk(x_sample, p_sample, lw, final_norm)
    return (y_prompt, y_sample)
```

```python
import functools
import math

import numpy as np
import jax
import jax.numpy as jnp
from jax import lax
from jax.experimental import pallas as pl
from jax.experimental.pallas import tpu as pltpu

D_MODEL = 1024
DEPTH = 4
GRID_W = 64
EPS = 1e-6
D_FF = 2816
D_PLE = 256
ROPE_BASE = 10000.0
A_HEADS = 8
A_KV_HEADS = 2
A_HEAD_DIM = 64
B_HEADS = 8
B_NOPE = 64
B_ROPE = 32
B_VDIM = 64
B_Q_RANK = 256
B_KV_RANK = 128
C_HEADS = 8
C_HEAD_DIM = 64

LANES = 128
VMEM_LIMIT = 56 * 1024 * 1024
NEG_BIG = -1e30
LOG2E = math.log2(math.e)
ONES_PAD = 16

F32 = jnp.float32
BF16 = jnp.bfloat16


def _cparams(n_axes):
    return pltpu.CompilerParams(
        dimension_semantics=("parallel",) * n_axes, vmem_limit_bytes=VMEM_LIMIT)


def _const_spec(shape):
    nd = len(shape)
    return pl.BlockSpec(shape, lambda *_: (0,) * nd, pipeline_mode=pl.Buffered(1))


def _rms(x, g):
    return x * lax.rsqrt(jnp.mean(x * x, axis=-1, keepdims=True) + EPS) * g


def _sigmoid(x):
    return 1.0 / (1.0 + jnp.exp(-x))


def _ffn_body(x_ref, g_ref, wg_ref, wu_ref, wd_ref, o_ref):
    x = x_ref[...]
    xn = _rms(x, g_ref[...]).astype(BF16)
    g = jnp.dot(xn, wg_ref[...], preferred_element_type=F32)
    u = jnp.dot(xn, wu_ref[...], preferred_element_type=F32)
    a = (g * _sigmoid(g) * u).astype(BF16)
    y = jnp.dot(a, wd_ref[...], preferred_element_type=F32)
    o_ref[...] = x + 0.5 * y


def _ffn(x, g, wg, wu, wd, tm):
    t, d = x.shape
    f = wg.shape[1]
    return pl.pallas_call(
        _ffn_body,
        grid=(t // tm,),
        in_specs=[
            pl.BlockSpec((tm, d), lambda i: (i, 0)),
            _const_spec((1, d)),
            _const_spec((d, f)),
            _const_spec((d, f)),
            _const_spec((f, d)),
        ],
        out_specs=pl.BlockSpec((tm, d), lambda i: (i, 0)),
        out_shape=jax.ShapeDtypeStruct((t, d), F32),
        compiler_params=_cparams(1),
        name="ffn",
    )(x, g, wg, wu, wd)


def _slab(ref_or_val, i):
    return ref_or_val[:, LANES * i:LANES * (i + 1)]


def _dot_nt(a, b):
    return lax.dot_general(a, b, (((1,), (1,)), ((), ())), preferred_element_type=F32)


def _values_t(wvt, h, dv):
    vt = _dot_nt(wvt, h)
    row = lax.broadcasted_iota(jnp.int32, vt.shape, 0)
    return jnp.where(row % (dv + ONES_PAD) == dv, 1.0, vt).astype(BF16)


def _inproj_even_body(x_ref, g_ref, w1_ref, wvat_ref, cosa_ref, sina_ref, cosb_ref, sinb_ref,
                      gq_ref, gqp_ref, gk_ref, gkp_ref, bqn_ref, wuq_ref, bkvn_ref, wukv_ref,
                      wvbt_ref, qa_ref, ka_ref, vat_ref, qb_ref, kb_ref, vbt_ref):
    h = _rms(x_ref[...], g_ref[...]).astype(BF16)
    z = jnp.dot(h, w1_ref[...], preferred_element_type=F32)
    vat_ref[...] = _values_t(wvat_ref[...], h, A_HEAD_DIM)
    cosa, sina = cosa_ref[...], sina_ref[...]
    cosb, sinb = cosb_ref[...], sinb_ref[...]

    scale_a = A_HEAD_DIM ** -0.5 * LOG2E
    cq, sq = cosa * (gq_ref[...] * scale_a), sina * (gqp_ref[...] * scale_a)
    ck, sk = cosa * gk_ref[...], sina * gkp_ref[...]
    low = lax.broadcasted_iota(jnp.int32, (x_ref.shape[0], LANES), 1) < A_HEAD_DIM

    def half_rsqrt(zz):
        sq2 = zz * zz
        lo = jnp.sum(jnp.where(low, sq2, 0.0), axis=-1, keepdims=True)
        hi = jnp.sum(jnp.where(low, 0.0, sq2), axis=-1, keepdims=True)
        return lax.rsqrt(jnp.where(low, lo, hi) * (1.0 / A_HEAD_DIM) + EPS)

    n_q, n_k = A_HEADS // 2, A_KV_HEADS
    for i in range(n_q):
        zz, zr = _slab(z, i), _slab(z, n_q + i)
        qa_ref[:, LANES * i:LANES * (i + 1)] = ((zz * cq + zr * sq) * half_rsqrt(zz)).astype(BF16)
    for i in range(n_k):
        zz, zr = _slab(z, 2 * n_q + i), _slab(z, 2 * n_q + n_k + i)
        ka_ref[:, LANES * i:LANES * (i + 1)] = ((zz * ck + zr * sk) * half_rsqrt(zz)).astype(BF16)

    c0 = LANES * (2 * n_q + 2 * n_k)
    scale_b = (B_NOPE + B_ROPE) ** -0.5 * LOG2E
    cqn = _rms(z[:, c0:c0 + B_Q_RANK], bqn_ref[...]).astype(BF16)
    zq = jnp.dot(cqn, wuq_ref[...], preferred_element_type=F32)
    for i in range(B_HEADS):
        qb_ref[:, LANES * i:LANES * (i + 1)] = (
            (_slab(zq, i) * cosb + _slab(zq, B_HEADS + i) * sinb) * scale_b).astype(BF16)
    c1 = c0 + B_Q_RANK
    ckvn = _rms(z[:, c1:c1 + B_KV_RANK], bkvn_ref[...]).astype(BF16)
    zk = jnp.dot(ckvn, wukv_ref[...], preferred_element_type=F32)
    vbt_ref[...] = _values_t(wvbt_ref[...], ckvn, B_VDIM)
    c2 = c1 + B_KV_RANK
    kr = z[:, c2:c2 + LANES] * cosb + z[:, c2 + LANES:c2 + 2 * LANES] * sinb
    for i in range(B_HEADS):
        kb_ref[:, LANES * i:LANES * (i + 1)] = (_slab(zk, i) + kr).astype(BF16)


def _inproj_even(x, g, w1, wvat, tabs, gq, gqp, gk, gkp, bqn, wuq, bkvn, wukv, wvbt, seq, tm):
    t, d = x.shape
    nt = seq // tm
    tab_spec = pl.BlockSpec((tm, LANES), lambda i: (i % nt, 0))
    row = lambda i: (i, 0)
    col = lambda i: (0, i)
    outs = ((A_HEADS // 2 * LANES, row), (A_KV_HEADS * LANES, row), (wvat.shape[0], col),
            (B_HEADS * LANES, row), (B_HEADS * LANES, row), (wvbt.shape[0], col))
    return pl.pallas_call(
        _inproj_even_body,
        grid=(t // tm,),
        in_specs=[pl.BlockSpec((tm, d), row), _const_spec((1, d)), _const_spec(w1.shape),
                  _const_spec(wvat.shape),
                  tab_spec, tab_spec, tab_spec, tab_spec,
                  _const_spec((1, LANES)), _const_spec((1, LANES)),
                  _const_spec((1, LANES)), _const_spec((1, LANES)),
                  _const_spec((1, B_Q_RANK)), _const_spec(wuq.shape),
                  _const_spec((1, B_KV_RANK)), _const_spec(wukv.shape),
                  _const_spec(wvbt.shape)],
        out_specs=[pl.BlockSpec((tm, n) if m is row else (n, tm), m) for n, m in outs],
        out_shape=[jax.ShapeDtypeStruct((t, n) if m is row else (n, t), BF16)
                   for n, m in outs],
        compiler_params=_cparams(1),
        name="inproj_even",
    )(x, g, w1, wvat, *tabs, gq, gqp, gk, gkp, bqn, wuq, bkvn, wukv, wvbt)


def _inproj_odd_body(x_ref, g_ref, w_ref, wvt_ref, q_ref, k_ref, vt_ref):
    h = _rms(x_ref[...], g_ref[...]).astype(BF16)
    z = jnp.dot(h, w_ref[...], preferred_element_type=F32)
    nq = C_HEADS * LANES
    q_ref[...] = (z[:, :nq] * (C_HEAD_DIM ** -0.5 * LOG2E)).astype(BF16)
    k_ref[...] = z[:, nq:].astype(BF16)
    vt_ref[...] = _values_t(wvt_ref[...], h, 2 * C_HEAD_DIM)


def _inproj_odd(x, g, w, wvt, tm):
    t, d = x.shape
    row = lambda i: (i, 0)
    return pl.pallas_call(
        _inproj_odd_body,
        grid=(t // tm,),
        in_specs=[pl.BlockSpec((tm, d), row), _const_spec((1, d)), _const_spec(w.shape),
                  _const_spec(wvt.shape)],
        out_specs=[pl.BlockSpec((tm, C_HEADS * LANES), row),
                   pl.BlockSpec((tm, C_HEADS * LANES), row),
                   pl.BlockSpec((wvt.shape[0], tm), lambda i: (0, i))],
        out_shape=[jax.ShapeDtypeStruct((t, C_HEADS * LANES), BF16),
                   jax.ShapeDtypeStruct((t, C_HEADS * LANES), BF16),
                   jax.ShapeDtypeStruct((wvt.shape[0], t), BF16)],
        compiler_params=_cparams(1),
        name="inproj_odd",
    )(x, g, w, wvt)


def _outproj_body(*refs):
    x_ref, o_ref = refs[0], refs[-1]
    acc = x_ref[...]
    pairs = refs[1:-1]
    for a_ref, w_ref in zip(pairs[0::2], pairs[1::2]):
        acc = acc + jnp.dot(a_ref[...], w_ref[...], preferred_element_type=F32)
    o_ref[...] = acc


def _outproj(x, pairs, tm):
    t, d = x.shape
    row = lambda i: (i, 0)
    in_specs = [pl.BlockSpec((tm, d), row)]
    args = [x]
    for a, w in pairs:
        in_specs += [pl.BlockSpec((tm, a.shape[1]), row), _const_spec(w.shape)]
        args += [a, w]
    return pl.pallas_call(
        _outproj_body,
        grid=(t // tm,),
        in_specs=in_specs,
        out_specs=pl.BlockSpec((tm, d), row),
        out_shape=jax.ShapeDtypeStruct((t, d), F32),
        compiler_params=_cparams(1),
        name="outproj",
    )(*args)


def _ple_body(x_ref, p_ref, g_ref, wgate_ref, wproj_ref, fg_ref, o_ref, *, final):
    x = x_ref[...]
    xn = _rms(x, g_ref[...]).astype(BF16)
    gate = _sigmoid(jnp.dot(xn, wgate_ref[...], preferred_element_type=F32))
    proj = jnp.dot(p_ref[...].astype(BF16), wproj_ref[...], preferred_element_type=F32)
    y = x + gate * proj
    if final:
        y = _rms(y, fg_ref[...])
    o_ref[...] = y


def _ple(x, p, g, wgate, wproj, fg, final, tm):
    t, d = x.shape
    row = lambda i: (i, 0)
    return pl.pallas_call(
        functools.partial(_ple_body, final=final),
        grid=(t // tm,),
        in_specs=[pl.BlockSpec((tm, d), row), pl.BlockSpec((tm, p.shape[1]), row),
                  _const_spec((1, d)), _const_spec(wgate.shape), _const_spec(wproj.shape),
                  _const_spec((1, d))],
        out_specs=pl.BlockSpec((tm, d), row),
        out_shape=jax.ShapeDtypeStruct((t, d), F32),
        compiler_params=_cparams(1),
        name="ple",
    )(x, p, g, wgate, wproj, fg)


def _attn_body(*refs, groups, packed, tq, tk, nt, seq, diff, lambda_init):
    if diff:
        (slopes_ref, q_ref, k_ref, vt_ref, lq1_ref, lk1_ref, lq2_ref, lk2_ref, subg_ref, o_ref,
         qs_ref, s_ref, p_ref, alpha_ref, m_ref, acc_ref, bias_ref) = refs
    else:
        q_ref, k_ref, vt_ref, o_ref, qs_ref, s_ref, p_ref, alpha_ref, m_ref, acc_ref = refs
    dv = vt_ref.shape[0] - ONES_PAD
    lane = lax.broadcasted_iota(jnp.int32, (tq, LANES), 1)
    for t in range(nt):
        for g in range(groups):
            if packed:
                qg = q_ref[0, t * tq:(t + 1) * tq, LANES * (g // 2):LANES * (g // 2 + 1)]
                qg = jnp.where((lane >= LANES // 2) == (g % 2 == 1), qg, jnp.zeros_like(qg))
            else:
                qg = q_ref[0, t * tq:(t + 1) * tq, LANES * g:LANES * (g + 1)]
            qs_ref[t, g * tq:(g + 1) * tq, :] = qg
    m_ref[...] = jnp.full(m_ref.shape, NEG_BIG, F32)
    acc_ref[...] = jnp.zeros(acc_ref.shape, F32)
    if diff:
        slope = slopes_ref[pl.program_id(1)] * LOG2E
        q_base = pl.program_id(2) * (nt * tq)
        rel = (lax.broadcasted_iota(jnp.int32, (tk, tq), 1)
               - lax.broadcasted_iota(jnp.int32, (tk, tq), 0)).astype(F32)
        bias_ref[0] = slope * rel
        bias_ref[1] = -slope * jnp.abs(rel)
        bias_ref[2] = -slope * rel

    def scores(item, slot):
        t, c = item
        s = _dot_nt(k_ref[0, c * tk:(c + 1) * tk, :], qs_ref[t])
        if diff:
            tile = bias_ref[jnp.sign(q_base + (t * tq - c * tk)) + 1]
            s = s + jnp.concatenate([tile] * groups, axis=1)
        s_ref[slot] = s

    def softmax(item, s_slot, slot):
        t, c = item
        s = s_ref[s_slot]
        m = m_ref[t]
        if diff:
            delta = q_base + (t * tq - c * tk)
            off = slope * jnp.abs(delta).astype(F32)
            m_new = jnp.maximum(m, jnp.max(s, axis=0, keepdims=True) - off)
            shift = m_new + off
        else:
            m_new = jnp.maximum(m, jnp.max(s, axis=0, keepdims=True))
            shift = m_new
        m_ref[t] = m_new
        alpha_ref[slot] = jnp.exp2(m - m_new)
        p_ref[slot] = jnp.exp2(s - shift).astype(BF16)

    def values(item, slot):
        t, c = item
        pv = jnp.dot(vt_ref[:, c * tk:(c + 1) * tk], p_ref[slot], preferred_element_type=F32)
        acc_ref[t] = alpha_ref[slot] * acc_ref[t] + pv

    def finalize(t):
        ot = acc_ref[t, :dv, :] / acc_ref[t, dv:dv + 1, :]
        rows = slice(t * tq, (t + 1) * tq)
        if diff:
            lam = (jnp.exp(jnp.sum(lq1_ref[...] * lk1_ref[...], keepdims=True))
                   - jnp.exp(jnp.sum(lq2_ref[...] * lk2_ref[...], keepdims=True)) + lambda_init)
            od = ot[:, :tq] - lam * ot[:, tq:]
            r = lax.rsqrt(jnp.mean(od * od, axis=0, keepdims=True) + EPS)
            od = od * r * (subg_ref[...] * (1.0 - lambda_init))
            o_ref[0, rows, :] = od.T.astype(BF16)
        else:
            pad = jnp.zeros((LANES - dv, tq), F32)
            for g in range(groups):
                og = jnp.concatenate([ot[:, g * tq:(g + 1) * tq], pad], axis=0)
                o_ref[0, rows, LANES * g:LANES * (g + 1)] = og.T.astype(BF16)

    n_chunks = seq // tk
    items = [(t, c) for t in range(nt) for c in range(n_chunks)]
    ahead = s_ref.shape[0]
    for u in range(min(ahead, len(items))):
        scores(items[u], u % ahead)
    softmax(items[0], 0, 0)
    for u, item in enumerate(items):
        if u + ahead < len(items):
            scores(items[u + ahead], u % ahead)
        values(item, u % 2)
        if item[1] == n_chunks - 1:
            finalize(item[0])
        if u + 1 < len(items):
            softmax(items[u + 1], (u + 1) % ahead, (u + 1) % 2)


def _attention(q, k, vt, groups, tiles, packed=False, diff_args=None, lambda_init=0.0):
    b, s, _ = q.shape
    nkv = k.shape[2] // LANES
    q_slabs = groups // 2 if packed else groups
    tq, tk, items = tiles
    tq, tk = min(tq, s), min(tk, s)
    nt = max(1, min(s // tq, items // (s // tk)))
    assert s % tk == 0 and s // tk >= 2 and s % (nt * tq) == 0, (s, tiles)
    cols = groups * tq
    vrows = vt.shape[0] // nkv
    scratch = [pltpu.VMEM((nt, cols, LANES), BF16),
               pltpu.VMEM((2, tk, cols), F32),
               pltpu.VMEM((2, tk, cols), BF16),
               pltpu.VMEM((2, 1, cols), F32),
               pltpu.VMEM((nt, 1, cols), F32),
               pltpu.VMEM((nt, vrows, cols), F32)]
    diff = diff_args is not None
    q_spec = pl.BlockSpec((1, nt * tq, q_slabs * LANES), lambda bi, h, i: (bi, i, h))
    k_spec = pl.BlockSpec((1, s, LANES), lambda bi, h, i: (bi, 0, h))
    vt_spec = pl.BlockSpec((vrows, s), lambda bi, h, i: (h, bi))
    in_specs = [q_spec, k_spec, vt_spec]
    args = [q, k, vt]
    out_w = 1 if diff else groups
    if diff:
        assert tq == tk, (tq, tk)
        scratch.append(pltpu.VMEM((3, tk, tq), F32))
        slopes, lq1, lk1, lq2, lk2, subg = diff_args
        in_specs = ([pl.BlockSpec(memory_space=pltpu.SMEM)] + in_specs
                    + [_const_spec((1, C_HEAD_DIM))] * 4 + [_const_spec((LANES, 1))])
        args = [slopes] + args + [lq1, lk1, lq2, lk2, subg]
    return pl.pallas_call(
        functools.partial(_attn_body, groups=groups, packed=packed, tq=tq, tk=tk, nt=nt, seq=s,
                          diff=diff, lambda_init=lambda_init),
        grid=(b, nkv, s // (nt * tq)),
        in_specs=in_specs,
        out_specs=pl.BlockSpec((1, nt * tq, out_w * LANES), lambda bi, h, i: (bi, i, h)),
        out_shape=jax.ShapeDtypeStruct((b, s, nkv * out_w * LANES), BF16),
        scratch_shapes=scratch,
        compiler_params=_cparams(3),
        name="attn_diff" if diff else f"attn_g{groups}",
    )(*args)


def _pad_slabs(w, n, width, offset=0):
    kdim = w.shape[0]
    w = w.reshape(kdim, n, width)
    w = jnp.pad(w, ((0, 0), (0, 0), (offset, LANES - width - offset)))
    return w.reshape(kdim, n * LANES)


def _rot_cols(w, n, perm, sign):
    kdim = w.shape[0]
    width = perm.shape[0]
    w = w.reshape(kdim, n, width)[:, :, perm] * sign
    return w.reshape(kdim, n * width)


_HALF = B_ROPE // 2
_PERM32 = np.concatenate([np.arange(_HALF, 2 * _HALF), np.arange(_HALF)])
_SIGN32 = np.concatenate([-np.ones(_HALF), np.ones(_HALF)]).astype(np.float32)
_PERM64 = np.concatenate([_PERM32, _PERM32 + 2 * _HALF])
_SIGN64 = np.concatenate([_SIGN32, _SIGN32])


def _rope_tables(seq):
    t = jnp.arange(seq, dtype=jnp.int32)
    inv = ROPE_BASE ** (-jnp.arange(_HALF, dtype=F32) * (2.0 / (2 * _HALF)))

    def cs(pos):
        ang = pos.astype(F32)[:, None] * inv[None, :]
        return jnp.cos(ang), jnp.sin(ang)

    cr, sr = cs(t // GRID_W)
    cc, sc = cs(t % GRID_W)
    ct, st = cs(t)
    cosa = jnp.concatenate([cr, cr, cc, cc] * (LANES // A_HEAD_DIM), axis=1)
    sina = jnp.concatenate([sr, sr, sc, sc] * (LANES // A_HEAD_DIM), axis=1)
    zb = jnp.zeros((seq, LANES - B_NOPE - B_ROPE), F32)
    cosb = jnp.concatenate([jnp.ones((seq, B_NOPE), F32), ct, ct, zb], axis=1)
    sinb = jnp.concatenate([jnp.zeros((seq, B_NOPE), F32), st, st, zb], axis=1)
    return cosa, sina, cosb, sinb


def _vt_weights(w, dv):
    kdim, n, _ = w.shape
    w = jnp.pad(w, ((0, 0), (0, 0), (0, ONES_PAD)))
    return w.reshape(kdim, n * (dv + ONES_PAD)).T.astype(BF16)


def _lane_row(g, perm=None):
    g = g.astype(F32)
    if perm is not None:
        g = g[perm]
    return jnp.tile(g, LANES // g.shape[0]).reshape(1, LANES)


def _prep_even(w_in, a_qn, a_kn, b_qn, b_wuq, b_kvn, b_wukv, w_out):
    na, nk = A_HEADS * A_HEAD_DIM, A_KV_HEADS * A_HEAD_DIM
    wq, wk, wv = w_in[:, :na], w_in[:, na:na + nk], w_in[:, na + nk:na + 2 * nk]
    c = na + 2 * nk
    wcq, wckv = w_in[:, c:c + B_Q_RANK], w_in[:, c + B_Q_RANK:c + B_Q_RANK + B_KV_RANK]
    wkr = w_in[:, c + B_Q_RANK + B_KV_RANK:]
    wk_rot = _rot_cols(wk, A_KV_HEADS, _PERM64, _SIGN64)
    twice = lambda w: jnp.tile(w.reshape(D_MODEL, A_KV_HEADS, 1, A_HEAD_DIM),
                               (1, 1, 2, 1)).reshape(D_MODEL, 2 * nk)
    w1 = jnp.concatenate([
        wq, _rot_cols(wq, A_HEADS, _PERM64, _SIGN64),
        twice(wk), twice(wk_rot),
        wcq, wckv,
        _pad_slabs(wkr, 1, B_ROPE, offset=B_NOPE),
        _pad_slabs(_rot_cols(wkr, 1, _PERM32, _SIGN32), 1, B_ROPE, offset=B_NOPE),
    ], axis=1).astype(BF16)
    dq = B_NOPE + B_ROPE
    uq = b_wuq.reshape(B_Q_RANK, B_HEADS, dq)
    uq_rot = uq[:, :, B_NOPE:][:, :, _PERM32] * _SIGN32
    wuq = jnp.concatenate([
        _pad_slabs(uq.reshape(B_Q_RANK, B_HEADS * dq), B_HEADS, dq),
        _pad_slabs(uq_rot.reshape(B_Q_RANK, B_HEADS * B_ROPE), B_HEADS, B_ROPE, offset=B_NOPE),
    ], axis=1).astype(BF16)
    ukv = b_wukv.reshape(B_KV_RANK, B_HEADS, B_NOPE + B_VDIM)
    wukv = _pad_slabs(ukv[:, :, :B_NOPE].reshape(B_KV_RANK, -1), B_HEADS, B_NOPE).astype(BF16)
    wvbt = _vt_weights(ukv[:, :, B_NOPE:], B_VDIM)
    wvat = _vt_weights(wv.reshape(D_MODEL, A_KV_HEADS, A_HEAD_DIM), A_HEAD_DIM)
    wo_a = jnp.pad(w_out[:na].reshape(A_HEADS, A_HEAD_DIM, D_MODEL),
                   ((0, 0), (0, LANES - A_HEAD_DIM), (0, 0))).reshape(A_HEADS * LANES, D_MODEL)
    wo_b = jnp.pad(w_out[na:].reshape(B_HEADS, B_VDIM, D_MODEL),
                   ((0, 0), (0, LANES - B_VDIM), (0, 0))).reshape(B_HEADS * LANES, D_MODEL)
    return dict(
        w1=w1, wvat=wvat, wuq=wuq, wukv=wukv, wvbt=wvbt, wo_a=wo_a.astype(BF16), wo_b=wo_b.astype(BF16),
        gq=_lane_row(a_qn), gqp=_lane_row(a_qn, _PERM64),
        gk=_lane_row(a_kn), gkp=_lane_row(a_kn, _PERM64),
        bqn=b_qn.reshape(1, -1), bkvn=b_kvn.reshape(1, -1))


def _prep_odd(w_in, w_out):
    n = C_HEADS * 2 * C_HEAD_DIM
    w = w_in[:, :2 * n].astype(BF16)
    wvt = _vt_weights(w_in[:, 2 * n:].reshape(D_MODEL, C_HEADS, 2 * C_HEAD_DIM), 2 * C_HEAD_DIM)
    return dict(w=w, wvt=wvt, wo=w_out.astype(BF16))


def _alibi_slopes():
    return 2.0 ** (-8.0 * jnp.arange(1, C_HEADS + 1, dtype=F32) / C_HEADS)


def _trunk(x, p, lw, final_norm, tm_ffn=512, tm_proj=512,
           tiles_a=(128, 256, 64), tiles_b=(512, 256, 64), tiles_c=(256, 256, 64)):
    b, s, d = x.shape
    t = b * s
    tm_ffn, tm_proj = min(tm_ffn, t), min(tm_proj, s)
    x = x.reshape(t, d)
    tabs = _rope_tables(s)
    slopes = _alibi_slopes()
    fg = final_norm.reshape(1, d)
    for i in range(DEPTH):
        w = lw[i]
        x = _ffn(x, w["ffn1_norm"], w["ffn1_wg"], w["ffn1_wu"], w["ffn1_wd"], tm_ffn)
        if i % 2 == 0:
            m = w["mix"]
            qa, ka, vat, qb, kb, vbt = _inproj_even(
                x, w["mix_norm"], m["w1"], m["wvat"], tabs, m["gq"], m["gqp"], m["gk"], m["gkp"],
                m["bqn"], m["wuq"], m["bkvn"], m["wukv"], m["wvbt"], s, tm_proj)
            r3 = lambda a: a.reshape(b, s, a.shape[1])
            oa = _attention(r3(qa), r3(ka), vat, A_HEADS // A_KV_HEADS, tiles_a, packed=True)
            ob = _attention(r3(qb), r3(kb), vbt, 1, tiles_b)
            x = _outproj(x, [(oa.reshape(t, -1), m["wo_a"]), (ob.reshape(t, -1), m["wo_b"])], tm_ffn)
        else:
            m = w["mix"]
            lambda_init = 0.8 - 0.6 * math.exp(-0.3 * i)
            q, k, vt = _inproj_odd(x, w["mix_norm"], m["w"], m["wvt"], tm_proj)
            r3 = lambda a: a.reshape(b, s, a.shape[1])
            oc = _attention(r3(q), r3(k), vt, 2, tiles_c, packed=True,
                            diff_args=(slopes, w["lq1"], w["lk1"], w["lq2"], w["lk2"], w["subg"]),
                            lambda_init=lambda_init)
            x = _outproj(x, [(oc.reshape(t, -1), m["wo"])], tm_ffn)
        x = _ffn(x, w["ffn2_norm"], w["ffn2_wg"], w["ffn2_wu"], w["ffn2_wd"], tm_ffn)
        x = _ple(x, p[i].reshape(t, -1), w["ple_norm"], w["ple_w_gate"], w["ple_w_proj"], fg,
                 i == DEPTH - 1, tm_ffn)
    return x.reshape(b, s, d)


def _layer_weights(ffn1_norm, ffn1_wg, ffn1_wu, ffn1_wd, mix_norm, ab_w_in, a_q_norm, a_k_norm,
                   b_q_norm, b_w_uq, b_kv_norm, b_w_ukv, ab_w_out, c_w_in, c_lambda_q1,
                   c_lambda_k1, c_lambda_q2, c_lambda_k2, c_sub_norm, c_w_out, ffn2_norm, ffn2_wg,
                   ffn2_wu, ffn2_wd, ple_norm, ple_w_gate, ple_w_proj):
    layers = []
    for i in range(DEPTH):
        w = dict(
            ffn1_norm=ffn1_norm[i].reshape(1, -1), ffn1_wg=ffn1_wg[i].astype(BF16),
            ffn1_wu=ffn1_wu[i].astype(BF16), ffn1_wd=ffn1_wd[i].astype(BF16),
            mix_norm=mix_norm[i].reshape(1, -1),
            ffn2_norm=ffn2_norm[i].reshape(1, -1), ffn2_wg=ffn2_wg[i].astype(BF16),
            ffn2_wu=ffn2_wu[i].astype(BF16), ffn2_wd=ffn2_wd[i].astype(BF16),
            ple_norm=ple_norm[i].reshape(1, -1), ple_w_gate=ple_w_gate[i].astype(BF16),
            ple_w_proj=ple_w_proj[i].astype(BF16))
        j = i // 2
        if i % 2 == 0:
            w["mix"] = _prep_even(ab_w_in[j], a_q_norm[j], a_k_norm[j], b_q_norm[j], b_w_uq[j],
                                  b_kv_norm[j], b_w_ukv[j], ab_w_out[j])
        else:
            w["mix"] = _prep_odd(c_w_in[j], c_w_out[j])
            w["lq1"] = c_lambda_q1[j].reshape(1, -1)
            w["lk1"] = c_lambda_k1[j].reshape(1, -1)
            w["lq2"] = c_lambda_q2[j].reshape(1, -1)
            w["lk2"] = c_lambda_k2[j].reshape(1, -1)
            w["subg"] = c_sub_norm[j].reshape(-1, 1)
        layers.append(w)
    return layers


def kernel(x_prompt, x_sample, p_prompt, p_sample, ffn1_norm, ffn1_wg, ffn1_wu, ffn1_wd, mix_norm, ab_w_in, a_q_norm, a_k_norm, b_q_norm, b_w_uq, b_kv_norm, b_w_ukv, ab_w_out, c_w_in, c_lambda_q1, c_lambda_k1, c_lambda_q2, c_lambda_k2, c_sub_norm, c_w_out, ffn2_norm, ffn2_wg, ffn2_wu, ffn2_wd, ple_norm, ple_w_gate, ple_w_proj, final_norm):
    lw = _layer_weights(ffn1_norm, ffn1_wg, ffn1_wu, ffn1_wd, mix_norm, ab_w_in, a_q_norm,
                        a_k_norm, b_q_norm, b_w_uq, b_kv_norm, b_w_ukv, ab_w_out, c_w_in,
                        c_lambda_q1, c_lambda_k1, c_lambda_q2, c_lambda_k2, c_sub_norm, c_w_out,
                        ffn2_norm, ffn2_wg, ffn2_wu, ffn2_wd, ple_norm, ple_w_gate, ple_w_proj)
    y_prompt = _trunk(x_prompt, p_prompt, lw, final_norm)
    y_sample = _trunk(x_sample, p_sample, lw, final_norm)
    return (y_prompt, y_sample)
```

```python
import functools
import math

import numpy as np
import jax
import jax.numpy as jnp
from jax import lax
from jax.experimental import pallas as pl
from jax.experimental.pallas import tpu as pltpu

D_MODEL = 1024
DEPTH = 4
GRID_W = 64
EPS = 1e-6
D_FF = 2816
D_PLE = 256
ROPE_BASE = 10000.0
A_HEADS = 8
A_KV_HEADS = 2
A_HEAD_DIM = 64
B_HEADS = 8
B_NOPE = 64
B_ROPE = 32
B_VDIM = 64
B_Q_RANK = 256
B_KV_RANK = 128
C_HEADS = 8
C_HEAD_DIM = 64

LANES = 128
VMEM_LIMIT = 56 * 1024 * 1024
NEG_BIG = -1e30
LOG2E = math.log2(math.e)
ONES_PAD = 16

F32 = jnp.float32
BF16 = jnp.bfloat16


def _cparams(n_axes):
    return pltpu.CompilerParams(
        dimension_semantics=("parallel",) * n_axes, vmem_limit_bytes=VMEM_LIMIT)


def _const_spec(shape):
    nd = len(shape)
    return pl.BlockSpec(shape, lambda *_: (0,) * nd, pipeline_mode=pl.Buffered(1))


def _rms(x, g):
    return x * lax.rsqrt(jnp.mean(x * x, axis=-1, keepdims=True) + EPS) * g


def _sigmoid(x):
    return 1.0 / (1.0 + jnp.exp(-x))


def _ffn_body(x_ref, g_ref, wg_ref, wu_ref, wd_ref, o_ref):
    x = x_ref[...]
    xn = _rms(x, g_ref[...]).astype(BF16)
    g = jnp.dot(xn, wg_ref[...], preferred_element_type=F32)
    u = jnp.dot(xn, wu_ref[...], preferred_element_type=F32)
    a = (g * _sigmoid(g) * u).astype(BF16)
    y = jnp.dot(a, wd_ref[...], preferred_element_type=F32)
    o_ref[...] = x + 0.5 * y


def _ffn(x, g, wg, wu, wd, tm):
    t, d = x.shape
    f = wg.shape[1]
    return pl.pallas_call(
        _ffn_body,
        grid=(t // tm,),
        in_specs=[
            pl.BlockSpec((tm, d), lambda i: (i, 0)),
            _const_spec((1, d)),
            _const_spec((d, f)),
            _const_spec((d, f)),
            _const_spec((f, d)),
        ],
        out_specs=pl.BlockSpec((tm, d), lambda i: (i, 0)),
        out_shape=jax.ShapeDtypeStruct((t, d), F32),
        compiler_params=_cparams(1),
        name="ffn",
    )(x, g, wg, wu, wd)


def _post_body(*refs, n_pairs, final):
    x_ref = refs[0]
    pairs = refs[1:1 + 2 * n_pairs]
    (g_ref, wg_ref, wu_ref, wd_ref, p_ref, pg_ref, wgate_ref, wproj_ref, fg_ref,
     o_ref) = refs[1 + 2 * n_pairs:]
    x = x_ref[...]
    for a_ref, w_ref in zip(pairs[0::2], pairs[1::2]):
        x = x + jnp.dot(a_ref[...], w_ref[...], preferred_element_type=F32)
    xn = _rms(x, g_ref[...]).astype(BF16)
    g = jnp.dot(xn, wg_ref[...], preferred_element_type=F32)
    u = jnp.dot(xn, wu_ref[...], preferred_element_type=F32)
    a = (g * _sigmoid(g) * u).astype(BF16)
    x = x + 0.5 * jnp.dot(a, wd_ref[...], preferred_element_type=F32)
    xn = _rms(x, pg_ref[...]).astype(BF16)
    gate = _sigmoid(jnp.dot(xn, wgate_ref[...], preferred_element_type=F32))
    proj = jnp.dot(p_ref[...].astype(BF16), wproj_ref[...], preferred_element_type=F32)
    y = x + gate * proj
    if final:
        y = _rms(y, fg_ref[...])
    o_ref[...] = y


def _post(x, pairs, g, wg, wu, wd, p, pg, wgate, wproj, fg, final, tm):
    t, d = x.shape
    row = lambda i: (i, 0)
    in_specs = [pl.BlockSpec((tm, d), row)]
    args = [x]
    for a, w in pairs:
        in_specs += [pl.BlockSpec((tm, a.shape[1]), row), _const_spec(w.shape)]
        args += [a, w]
    in_specs += [_const_spec((1, d)), _const_spec(wg.shape), _const_spec(wu.shape),
                 _const_spec(wd.shape), pl.BlockSpec((tm, p.shape[1]), row), _const_spec((1, d)),
                 _const_spec(wgate.shape), _const_spec(wproj.shape), _const_spec((1, d))]
    args += [g, wg, wu, wd, p, pg, wgate, wproj, fg]
    return pl.pallas_call(
        functools.partial(_post_body, n_pairs=len(pairs), final=final),
        grid=(t // tm,),
        in_specs=in_specs,
        out_specs=pl.BlockSpec((tm, d), row),
        out_shape=jax.ShapeDtypeStruct((t, d), F32),
        compiler_params=_cparams(1),
        name="post",
    )(*args)


def _slab(ref_or_val, i):
    return ref_or_val[:, LANES * i:LANES * (i + 1)]


def _dot_nt(a, b):
    return lax.dot_general(a, b, (((1,), (1,)), ((), ())), preferred_element_type=F32)


def _values_t(wvt, h, dv):
    vt = _dot_nt(wvt, h)
    row = lax.broadcasted_iota(jnp.int32, vt.shape, 0)
    return jnp.where(row % (dv + ONES_PAD) == dv, 1.0, vt).astype(BF16)


def _inproj_even_body(x_ref, g_ref, w1_ref, wvat_ref, cosa_ref, sina_ref, cosb_ref, sinb_ref,
                      gq_ref, gqp_ref, gk_ref, gkp_ref, bqn_ref, wuq_ref, bkvn_ref, wukv_ref,
                      wvbt_ref, qa_ref, ka_ref, vat_ref, qb_ref, kb_ref, vbt_ref):
    h = _rms(x_ref[...], g_ref[...]).astype(BF16)
    z = jnp.dot(h, w1_ref[...], preferred_element_type=F32)
    vat_ref[...] = _values_t(wvat_ref[...], h, A_HEAD_DIM)
    cosa, sina = cosa_ref[...], sina_ref[...]
    cosb, sinb = cosb_ref[...], sinb_ref[...]

    scale_a = A_HEAD_DIM ** -0.5 * LOG2E
    cq, sq = cosa * (gq_ref[...] * scale_a), sina * (gqp_ref[...] * scale_a)
    ck, sk = cosa * gk_ref[...], sina * gkp_ref[...]
    low = lax.broadcasted_iota(jnp.int32, (x_ref.shape[0], LANES), 1) < A_HEAD_DIM

    def half_rsqrt(zz):
        sq2 = zz * zz
        lo = jnp.sum(jnp.where(low, sq2, 0.0), axis=-1, keepdims=True)
        hi = jnp.sum(jnp.where(low, 0.0, sq2), axis=-1, keepdims=True)
        return lax.rsqrt(jnp.where(low, lo, hi) * (1.0 / A_HEAD_DIM) + EPS)

    n_q, n_k = A_HEADS // 2, A_KV_HEADS
    for i in range(n_q):
        zz, zr = _slab(z, i), _slab(z, n_q + i)
        qa_ref[:, LANES * i:LANES * (i + 1)] = ((zz * cq + zr * sq) * half_rsqrt(zz)).astype(BF16)
    for i in range(n_k):
        zz, zr = _slab(z, 2 * n_q + i), _slab(z, 2 * n_q + n_k + i)
        ka_ref[:, LANES * i:LANES * (i + 1)] = ((zz * ck + zr * sk) * half_rsqrt(zz)).astype(BF16)

    c0 = LANES * (2 * n_q + 2 * n_k)
    scale_b = (B_NOPE + B_ROPE) ** -0.5 * LOG2E
    cqn = _rms(z[:, c0:c0 + B_Q_RANK], bqn_ref[...]).astype(BF16)
    zq = jnp.dot(cqn, wuq_ref[...], preferred_element_type=F32)
    for i in range(B_HEADS):
        qb_ref[:, LANES * i:LANES * (i + 1)] = (
            (_slab(zq, i) * cosb + _slab(zq, B_HEADS + i) * sinb) * scale_b).astype(BF16)
    c1 = c0 + B_Q_RANK
    ckvn = _rms(z[:, c1:c1 + B_KV_RANK], bkvn_ref[...]).astype(BF16)
    zk = jnp.dot(ckvn, wukv_ref[...], preferred_element_type=F32)
    vbt_ref[...] = _values_t(wvbt_ref[...], ckvn, B_VDIM)
    c2 = c1 + B_KV_RANK
    kr = z[:, c2:c2 + LANES] * cosb + z[:, c2 + LANES:c2 + 2 * LANES] * sinb
    for i in range(B_HEADS):
        kb_ref[:, LANES * i:LANES * (i + 1)] = (_slab(zk, i) + kr).astype(BF16)


def _inproj_even(x, g, w1, wvat, tabs, gq, gqp, gk, gkp, bqn, wuq, bkvn, wukv, wvbt, seq, tm):
    t, d = x.shape
    nt = seq // tm
    tab_spec = pl.BlockSpec((tm, LANES), lambda i: (i % nt, 0))
    row = lambda i: (i, 0)
    col = lambda i: (0, i)
    outs = ((A_HEADS // 2 * LANES, row), (A_KV_HEADS * LANES, row), (wvat.shape[0], col),
            (B_HEADS * LANES, row), (B_HEADS * LANES, row), (wvbt.shape[0], col))
    return pl.pallas_call(
        _inproj_even_body,
        grid=(t // tm,),
        in_specs=[pl.BlockSpec((tm, d), row), _const_spec((1, d)), _const_spec(w1.shape),
                  _const_spec(wvat.shape),
                  tab_spec, tab_spec, tab_spec, tab_spec,
                  _const_spec((1, LANES)), _const_spec((1, LANES)),
                  _const_spec((1, LANES)), _const_spec((1, LANES)),
                  _const_spec((1, B_Q_RANK)), _const_spec(wuq.shape),
                  _const_spec((1, B_KV_RANK)), _const_spec(wukv.shape),
                  _const_spec(wvbt.shape)],
        out_specs=[pl.BlockSpec((tm, n) if m is row else (n, tm), m) for n, m in outs],
        out_shape=[jax.ShapeDtypeStruct((t, n) if m is row else (n, t), BF16)
                   for n, m in outs],
        compiler_params=_cparams(1),
        name="inproj_even",
    )(x, g, w1, wvat, *tabs, gq, gqp, gk, gkp, bqn, wuq, bkvn, wukv, wvbt)


def _inproj_odd_body(x_ref, g_ref, w_ref, wvt_ref, q_ref, k_ref, vt_ref):
    h = _rms(x_ref[...], g_ref[...]).astype(BF16)
    z = jnp.dot(h, w_ref[...], preferred_element_type=F32)
    nq = C_HEADS * LANES
    q_ref[...] = (z[:, :nq] * (C_HEAD_DIM ** -0.5 * LOG2E)).astype(BF16)
    k_ref[...] = z[:, nq:].astype(BF16)
    vt_ref[...] = _values_t(wvt_ref[...], h, 2 * C_HEAD_DIM)


def _inproj_odd(x, g, w, wvt, tm):
    t, d = x.shape
    row = lambda i: (i, 0)
    return pl.pallas_call(
        _inproj_odd_body,
        grid=(t // tm,),
        in_specs=[pl.BlockSpec((tm, d), row), _const_spec((1, d)), _const_spec(w.shape),
                  _const_spec(wvt.shape)],
        out_specs=[pl.BlockSpec((tm, C_HEADS * LANES), row),
                   pl.BlockSpec((tm, C_HEADS * LANES), row),
                   pl.BlockSpec((wvt.shape[0], tm), lambda i: (0, i))],
        out_shape=[jax.ShapeDtypeStruct((t, C_HEADS * LANES), BF16),
                   jax.ShapeDtypeStruct((t, C_HEADS * LANES), BF16),
                   jax.ShapeDtypeStruct((wvt.shape[0], t), BF16)],
        compiler_params=_cparams(1),
        name="inproj_odd",
    )(x, g, w, wvt)


def _attn_body(*refs, groups, packed, tq, tk, nt, seq, diff, lambda_init):
    if diff:
        (slopes_ref, q_ref, k_ref, vt_ref, lq1_ref, lk1_ref, lq2_ref, lk2_ref, subg_ref, o_ref,
         qs_ref, s_ref, p_ref, alpha_ref, m_ref, acc_ref, bias_ref) = refs
    else:
        q_ref, k_ref, vt_ref, o_ref, qs_ref, s_ref, p_ref, alpha_ref, m_ref, acc_ref = refs
    dv = vt_ref.shape[0] - ONES_PAD
    lane = lax.broadcasted_iota(jnp.int32, (tq, LANES), 1)
    for t in range(nt):
        for g in range(groups):
            if packed:
                qg = q_ref[0, t * tq:(t + 1) * tq, LANES * (g // 2):LANES * (g // 2 + 1)]
                qg = jnp.where((lane >= LANES // 2) == (g % 2 == 1), qg, jnp.zeros_like(qg))
            else:
                qg = q_ref[0, t * tq:(t + 1) * tq, LANES * g:LANES * (g + 1)]
            qs_ref[t, g * tq:(g + 1) * tq, :] = qg
    m_ref[...] = jnp.full(m_ref.shape, NEG_BIG, F32)
    acc_ref[...] = jnp.zeros(acc_ref.shape, F32)
    if diff:
        slope = slopes_ref[pl.program_id(1)] * LOG2E
        q_base = pl.program_id(2) * (nt * tq)
        rel = (lax.broadcasted_iota(jnp.int32, (tk, tq), 1)
               - lax.broadcasted_iota(jnp.int32, (tk, tq), 0)).astype(F32)
        bias_ref[0] = slope * rel
        bias_ref[1] = -slope * jnp.abs(rel)
        bias_ref[2] = -slope * rel

    def scores(item, slot):
        t, c = item
        s = _dot_nt(k_ref[0, c * tk:(c + 1) * tk, :], qs_ref[t])
        if diff:
            tile = bias_ref[jnp.sign(q_base + (t * tq - c * tk)) + 1]
            s = s + jnp.concatenate([tile] * groups, axis=1)
        s_ref[slot] = s

    def softmax(item, s_slot, slot):
        t, c = item
        s = s_ref[s_slot]
        m = m_ref[t]
        if diff:
            delta = q_base + (t * tq - c * tk)
            off = slope * jnp.abs(delta).astype(F32)
            m_new = jnp.maximum(m, jnp.max(s, axis=0, keepdims=True) - off)
            shift = m_new + off
        else:
            m_new = jnp.maximum(m, jnp.max(s, axis=0, keepdims=True))
            shift = m_new
        m_ref[t] = m_new
        alpha_ref[slot] = jnp.exp2(m - m_new)
        p_ref[slot] = jnp.exp2(s - shift).astype(BF16)

    def values(item, slot):
        t, c = item
        pv = jnp.dot(vt_ref[:, c * tk:(c + 1) * tk], p_ref[slot], preferred_element_type=F32)
        acc_ref[t] = alpha_ref[slot] * acc_ref[t] + pv

    def finalize(t):
        ot = acc_ref[t, :dv, :] / acc_ref[t, dv:dv + 1, :]
        rows = slice(t * tq, (t + 1) * tq)
        if diff:
            lam = (jnp.exp(jnp.sum(lq1_ref[...] * lk1_ref[...], keepdims=True))
                   - jnp.exp(jnp.sum(lq2_ref[...] * lk2_ref[...], keepdims=True)) + lambda_init)
            od = ot[:, :tq] - lam * ot[:, tq:]
            r = lax.rsqrt(jnp.mean(od * od, axis=0, keepdims=True) + EPS)
            od = od * r * (subg_ref[...] * (1.0 - lambda_init))
            o_ref[0, rows, :] = od.T.astype(BF16)
        else:
            pad = jnp.zeros((LANES - dv, tq), F32)
            for g in range(groups):
                og = jnp.concatenate([ot[:, g * tq:(g + 1) * tq], pad], axis=0)
                o_ref[0, rows, LANES * g:LANES * (g + 1)] = og.T.astype(BF16)

    n_chunks = seq // tk
    items = [(t, c) for t in range(nt) for c in range(n_chunks)]
    ahead = s_ref.shape[0]
    for u in range(min(ahead, len(items))):
        scores(items[u], u % ahead)
    softmax(items[0], 0, 0)
    for u, item in enumerate(items):
        if u + ahead < len(items):
            scores(items[u + ahead], u % ahead)
        values(item, u % 2)
        if item[1] == n_chunks - 1:
            finalize(item[0])
        if u + 1 < len(items):
            softmax(items[u + 1], (u + 1) % ahead, (u + 1) % 2)


def _attention(q, k, vt, groups, tiles, packed=False, diff_args=None, lambda_init=0.0):
    b, s, _ = q.shape
    nkv = k.shape[2] // LANES
    q_slabs = groups // 2 if packed else groups
    tq, tk, items = tiles
    tq, tk = min(tq, s), min(tk, s)
    nt = max(1, min(s // tq, items // (s // tk)))
    assert s % tk == 0 and s // tk >= 2 and s % (nt * tq) == 0, (s, tiles)
    cols = groups * tq
    vrows = vt.shape[0] // nkv
    scratch = [pltpu.VMEM((nt, cols, LANES), BF16),
               pltpu.VMEM((2, tk, cols), F32),
               pltpu.VMEM((2, tk, cols), BF16),
               pltpu.VMEM((2, 1, cols), F32),
               pltpu.VMEM((nt, 1, cols), F32),
               pltpu.VMEM((nt, vrows, cols), F32)]
    diff = diff_args is not None
    q_spec = pl.BlockSpec((1, nt * tq, q_slabs * LANES), lambda bi, h, i: (bi, i, h))
    k_spec = pl.BlockSpec((1, s, LANES), lambda bi, h, i: (bi, 0, h))
    vt_spec = pl.BlockSpec((vrows, s), lambda bi, h, i: (h, bi))
    in_specs = [q_spec, k_spec, vt_spec]
    args = [q, k, vt]
    out_w = 1 if diff else groups
    if diff:
        assert tq == tk, (tq, tk)
        scratch.append(pltpu.VMEM((3, tk, tq), F32))
        slopes, lq1, lk1, lq2, lk2, subg = diff_args
        in_specs = ([pl.BlockSpec(memory_space=pltpu.SMEM)] + in_specs
                    + [_const_spec((1, C_HEAD_DIM))] * 4 + [_const_spec((LANES, 1))])
        args = [slopes] + args + [lq1, lk1, lq2, lk2, subg]
    return pl.pallas_call(
        functools.partial(_attn_body, groups=groups, packed=packed, tq=tq, tk=tk, nt=nt, seq=s,
                          diff=diff, lambda_init=lambda_init),
        grid=(b, nkv, s // (nt * tq)),
        in_specs=in_specs,
        out_specs=pl.BlockSpec((1, nt * tq, out_w * LANES), lambda bi, h, i: (bi, i, h)),
        out_shape=jax.ShapeDtypeStruct((b, s, nkv * out_w * LANES), BF16),
        scratch_shapes=scratch,
        compiler_params=_cparams(3),
        name="attn_diff" if diff else f"attn_g{groups}",
    )(*args)


def _pad_slabs(w, n, width, offset=0):
    kdim = w.shape[0]
    w = w.reshape(kdim, n, width)
    w = jnp.pad(w, ((0, 0), (0, 0), (offset, LANES - width - offset)))
    return w.reshape(kdim, n * LANES)


def _rot_cols(w, n, perm, sign):
    kdim = w.shape[0]
    width = perm.shape[0]
    w = w.reshape(kdim, n, width)[:, :, perm] * sign
    return w.reshape(kdim, n * width)


_HALF = B_ROPE // 2
_PERM32 = np.concatenate([np.arange(_HALF, 2 * _HALF), np.arange(_HALF)])
_SIGN32 = np.concatenate([-np.ones(_HALF), np.ones(_HALF)]).astype(np.float32)
_PERM64 = np.concatenate([_PERM32, _PERM32 + 2 * _HALF])
_SIGN64 = np.concatenate([_SIGN32, _SIGN32])


def _rope_tables(seq):
    t = jnp.arange(seq, dtype=jnp.int32)
    inv = ROPE_BASE ** (-jnp.arange(_HALF, dtype=F32) * (2.0 / (2 * _HALF)))

    def cs(pos):
        ang = pos.astype(F32)[:, None] * inv[None, :]
        return jnp.cos(ang), jnp.sin(ang)

    cr, sr = cs(t // GRID_W)
    cc, sc = cs(t % GRID_W)
    ct, st = cs(t)
    cosa = jnp.concatenate([cr, cr, cc, cc] * (LANES // A_HEAD_DIM), axis=1)
    sina = jnp.concatenate([sr, sr, sc, sc] * (LANES // A_HEAD_DIM), axis=1)
    zb = jnp.zeros((seq, LANES - B_NOPE - B_ROPE), F32)
    cosb = jnp.concatenate([jnp.ones((seq, B_NOPE), F32), ct, ct, zb], axis=1)
    sinb = jnp.concatenate([jnp.zeros((seq, B_NOPE), F32), st, st, zb], axis=1)
    return cosa, sina, cosb, sinb


def _vt_weights(w, dv):
    kdim, n, _ = w.shape
    w = jnp.pad(w, ((0, 0), (0, 0), (0, ONES_PAD)))
    return w.reshape(kdim, n * (dv + ONES_PAD)).T.astype(BF16)


def _lane_row(g, perm=None):
    g = g.astype(F32)
    if perm is not None:
        g = g[perm]
    return jnp.tile(g, LANES // g.shape[0]).reshape(1, LANES)


def _prep_even(w_in, a_qn, a_kn, b_qn, b_wuq, b_kvn, b_wukv, w_out):
    na, nk = A_HEADS * A_HEAD_DIM, A_KV_HEADS * A_HEAD_DIM
    wq, wk, wv = w_in[:, :na], w_in[:, na:na + nk], w_in[:, na + nk:na + 2 * nk]
    c = na + 2 * nk
    wcq, wckv = w_in[:, c:c + B_Q_RANK], w_in[:, c + B_Q_RANK:c + B_Q_RANK + B_KV_RANK]
    wkr = w_in[:, c + B_Q_RANK + B_KV_RANK:]
    wk_rot = _rot_cols(wk, A_KV_HEADS, _PERM64, _SIGN64)
    twice = lambda w: jnp.tile(w.reshape(D_MODEL, A_KV_HEADS, 1, A_HEAD_DIM),
                               (1, 1, 2, 1)).reshape(D_MODEL, 2 * nk)
    w1 = jnp.concatenate([
        wq, _rot_cols(wq, A_HEADS, _PERM64, _SIGN64),
        twice(wk), twice(wk_rot),
        wcq, wckv,
        _pad_slabs(wkr, 1, B_ROPE, offset=B_NOPE),
        _pad_slabs(_rot_cols(wkr, 1, _PERM32, _SIGN32), 1, B_ROPE, offset=B_NOPE),
    ], axis=1).astype(BF16)
    dq = B_NOPE + B_ROPE
    uq = b_wuq.reshape(B_Q_RANK, B_HEADS, dq)
    uq_rot = uq[:, :, B_NOPE:][:, :, _PERM32] * _SIGN32
    wuq = jnp.concatenate([
        _pad_slabs(uq.reshape(B_Q_RANK, B_HEADS * dq), B_HEADS, dq),
        _pad_slabs(uq_rot.reshape(B_Q_RANK, B_HEADS * B_ROPE), B_HEADS, B_ROPE, offset=B_NOPE),
    ], axis=1).astype(BF16)
    ukv = b_wukv.reshape(B_KV_RANK, B_HEADS, B_NOPE + B_VDIM)
    wukv = _pad_slabs(ukv[:, :, :B_NOPE].reshape(B_KV_RANK, -1), B_HEADS, B_NOPE).astype(BF16)
    wvbt = _vt_weights(ukv[:, :, B_NOPE:], B_VDIM)
    wvat = _vt_weights(wv.reshape(D_MODEL, A_KV_HEADS, A_HEAD_DIM), A_HEAD_DIM)
    wo_a = jnp.pad(w_out[:na].reshape(A_HEADS, A_HEAD_DIM, D_MODEL),
                   ((0, 0), (0, LANES - A_HEAD_DIM), (0, 0))).reshape(A_HEADS * LANES, D_MODEL)
    wo_b = jnp.pad(w_out[na:].reshape(B_HEADS, B_VDIM, D_MODEL),
                   ((0, 0), (0, LANES - B_VDIM), (0, 0))).reshape(B_HEADS * LANES, D_MODEL)
    return dict(
        w1=w1, wvat=wvat, wuq=wuq, wukv=wukv, wvbt=wvbt, wo_a=wo_a.astype(BF16), wo_b=wo_b.astype(BF16),
        gq=_lane_row(a_qn), gqp=_lane_row(a_qn, _PERM64),
        gk=_lane_row(a_kn), gkp=_lane_row(a_kn, _PERM64),
        bqn=b_qn.reshape(1, -1), bkvn=b_kvn.reshape(1, -1))


def _prep_odd(w_in, w_out):
    n = C_HEADS * 2 * C_HEAD_DIM
    w = w_in[:, :2 * n].astype(BF16)
    wvt = _vt_weights(w_in[:, 2 * n:].reshape(D_MODEL, C_HEADS, 2 * C_HEAD_DIM), 2 * C_HEAD_DIM)
    return dict(w=w, wvt=wvt, wo=w_out.astype(BF16))


def _alibi_slopes():
    return 2.0 ** (-8.0 * jnp.arange(1, C_HEADS + 1, dtype=F32) / C_HEADS)


def _trunk(x, p, lw, final_norm, tm_ffn=512, tm_proj=512,
           tiles_a=(128, 256, 64), tiles_b=(512, 256, 64), tiles_c=(256, 256, 64)):
    b, s, d = x.shape
    t = b * s
    tm_ffn, tm_proj = min(tm_ffn, t), min(tm_proj, s)
    x = x.reshape(t, d)
    tabs = _rope_tables(s)
    slopes = _alibi_slopes()
    fg = final_norm.reshape(1, d)
    for i in range(DEPTH):
        w = lw[i]
        x = _ffn(x, w["ffn1_norm"], w["ffn1_wg"], w["ffn1_wu"], w["ffn1_wd"], tm_ffn)
        if i % 2 == 0:
            m = w["mix"]
            qa, ka, vat, qb, kb, vbt = _inproj_even(
                x, w["mix_norm"], m["w1"], m["wvat"], tabs, m["gq"], m["gqp"], m["gk"], m["gkp"],
                m["bqn"], m["wuq"], m["bkvn"], m["wukv"], m["wvbt"], s, tm_proj)
            r3 = lambda a: a.reshape(b, s, a.shape[1])
            oa = _attention(r3(qa), r3(ka), vat, A_HEADS // A_KV_HEADS, tiles_a, packed=True)
            ob = _attention(r3(qb), r3(kb), vbt, 1, tiles_b)
            mixed = [(oa.reshape(t, -1), m["wo_a"]), (ob.reshape(t, -1), m["wo_b"])]
        else:
            m = w["mix"]
            lambda_init = 0.8 - 0.6 * math.exp(-0.3 * i)
            q, k, vt = _inproj_odd(x, w["mix_norm"], m["w"], m["wvt"], tm_proj)
            r3 = lambda a: a.reshape(b, s, a.shape[1])
            oc = _attention(r3(q), r3(k), vt, 2, tiles_c, packed=True,
                            diff_args=(slopes, w["lq1"], w["lk1"], w["lq2"], w["lk2"], w["subg"]),
                            lambda_init=lambda_init)
            mixed = [(oc.reshape(t, -1), m["wo"])]
        x = _post(x, mixed, w["ffn2_norm"], w["ffn2_wg"], w["ffn2_wu"], w["ffn2_wd"],
                  p[i].reshape(t, -1), w["ple_norm"], w["ple_w_gate"], w["ple_w_proj"], fg,
                  i == DEPTH - 1, tm_ffn)
    return x.reshape(b, s, d)


def _layer_weights(ffn1_norm, ffn1_wg, ffn1_wu, ffn1_wd, mix_norm, ab_w_in, a_q_norm, a_k_norm,
                   b_q_norm, b_w_uq, b_kv_norm, b_w_ukv, ab_w_out, c_w_in, c_lambda_q1,
                   c_lambda_k1, c_lambda_q2, c_lambda_k2, c_sub_norm, c_w_out, ffn2_norm, ffn2_wg,
                   ffn2_wu, ffn2_wd, ple_norm, ple_w_gate, ple_w_proj):
    layers = []
    for i in range(DEPTH):
        w = dict(
            ffn1_norm=ffn1_norm[i].reshape(1, -1), ffn1_wg=ffn1_wg[i].astype(BF16),
            ffn1_wu=ffn1_wu[i].astype(BF16), ffn1_wd=ffn1_wd[i].astype(BF16),
            mix_norm=mix_norm[i].reshape(1, -1),
            ffn2_norm=ffn2_norm[i].reshape(1, -1), ffn2_wg=ffn2_wg[i].astype(BF16),
            ffn2_wu=ffn2_wu[i].astype(BF16), ffn2_wd=ffn2_wd[i].astype(BF16),
            ple_norm=ple_norm[i].reshape(1, -1), ple_w_gate=ple_w_gate[i].astype(BF16),
            ple_w_proj=ple_w_proj[i].astype(BF16))
        j = i // 2
        if i % 2 == 0:
            w["mix"] = _prep_even(ab_w_in[j], a_q_norm[j], a_k_norm[j], b_q_norm[j], b_w_uq[j],
                                  b_kv_norm[j], b_w_ukv[j], ab_w_out[j])
        else:
            w["mix"] = _prep_odd(c_w_in[j], c_w_out[j])
            w["lq1"] = c_lambda_q1[j].reshape(1, -1)
            w["lk1"] = c_lambda_k1[j].reshape(1, -1)
            w["lq2"] = c_lambda_q2[j].reshape(1, -1)
            w["lk2"] = c_lambda_k2[j].reshape(1, -1)
            w["subg"] = c_sub_norm[j].reshape(-1, 1)
        layers.append(w)
    return layers


def kernel(x_prompt, x_sample, p_prompt, p_sample, ffn1_norm, ffn1_wg, ffn1_wu, ffn1_wd, mix_norm, ab_w_in, a_q_norm, a_k_norm, b_q_norm, b_w_uq, b_kv_norm, b_w_ukv, ab_w_out, c_w_in, c_lambda_q1, c_lambda_k1, c_lambda_q2, c_lambda_k2, c_sub_norm, c_w_out, ffn2_norm, ffn2_wg, ffn2_wu, ffn2_wd, ple_norm, ple_w_gate, ple_w_proj, final_norm):
    lw = _layer_weights(ffn1_norm, ffn1_wg, ffn1_wu, ffn1_wd, mix_norm, ab_w_in, a_q_norm,
                        a_k_norm, b_q_norm, b_w_uq, b_kv_norm, b_w_ukv, ab_w_out, c_w_in,
                        c_lambda_q1, c_lambda_k1, c_lambda_q2, c_lambda_k2, c_sub_norm, c_w_out,
                        ffn2_norm, ffn2_wg, ffn2_wu, ffn2_wd, ple_norm, ple_w_gate, ple_w_proj)
    y_prompt = _trunk(x_prompt, p_prompt, lw, final_norm)
    y_sample = _trunk(x_sample, p_sample, lw, final_norm)
    return (y_prompt, y_sample)
```

```python
import functools
import math

import numpy as np
import jax
import jax.numpy as jnp
from jax import lax
from jax.experimental import pallas as pl
from jax.experimental.pallas import tpu as pltpu

D_MODEL = 1024
DEPTH = 4
GRID_W = 64
EPS = 1e-6
D_FF = 2816
D_PLE = 256
ROPE_BASE = 10000.0
A_HEADS = 8
A_KV_HEADS = 2
A_HEAD_DIM = 64
B_HEADS = 8
B_NOPE = 64
B_ROPE = 32
B_VDIM = 64
B_Q_RANK = 256
B_KV_RANK = 128
C_HEADS = 8
C_HEAD_DIM = 64

LANES = 128
VMEM_LIMIT = 56 * 1024 * 1024
NEG_BIG = -1e30
LOG2E = math.log2(math.e)
ONES_PAD = 16

F32 = jnp.float32
BF16 = jnp.bfloat16


def _cparams(n_axes):
    return pltpu.CompilerParams(
        dimension_semantics=("parallel",) * n_axes, vmem_limit_bytes=VMEM_LIMIT)


def _const_spec(shape):
    nd = len(shape)
    return pl.BlockSpec(shape, lambda *_: (0,) * nd, pipeline_mode=pl.Buffered(1))


def _rms(x, g):
    return x * lax.rsqrt(jnp.mean(x * x, axis=-1, keepdims=True) + EPS) * g


def _sigmoid(x):
    return 1.0 / (1.0 + jnp.exp(-x))


def _half_swiglu(x, g_ref, wg_ref, wu_ref, wd_ref):
    xn = _rms(x, g_ref[...]).astype(BF16)
    g = jnp.dot(xn, wg_ref[...], preferred_element_type=F32)
    u = jnp.dot(xn, wu_ref[...], preferred_element_type=F32)
    a = (g * _sigmoid(g) * u).astype(BF16)
    return x + 0.5 * jnp.dot(a, wd_ref[...], preferred_element_type=F32)


def _ffn_specs(ffn):
    return [_const_spec(a.shape) for a in ffn]


def _post_body(*refs, n_pairs, final):
    x_ref = refs[0]
    pairs = refs[1:1 + 2 * n_pairs]
    (g_ref, wg_ref, wu_ref, wd_ref, p_ref, pg_ref, wgate_ref, wproj_ref, fg_ref,
     o_ref) = refs[1 + 2 * n_pairs:]
    x = x_ref[...]
    for a_ref, w_ref in zip(pairs[0::2], pairs[1::2]):
        x = x + jnp.dot(a_ref[...], w_ref[...], preferred_element_type=F32)
    x = _half_swiglu(x, g_ref, wg_ref, wu_ref, wd_ref)
    xn = _rms(x, pg_ref[...]).astype(BF16)
    gate = _sigmoid(jnp.dot(xn, wgate_ref[...], preferred_element_type=F32))
    proj = jnp.dot(p_ref[...].astype(BF16), wproj_ref[...], preferred_element_type=F32)
    y = x + gate * proj
    if final:
        y = _rms(y, fg_ref[...])
    o_ref[...] = y


def _post(x, pairs, g, wg, wu, wd, p, pg, wgate, wproj, fg, final, tm):
    t, d = x.shape
    row = lambda i: (i, 0)
    in_specs = [pl.BlockSpec((tm, d), row)]
    args = [x]
    for a, w in pairs:
        in_specs += [pl.BlockSpec((tm, a.shape[1]), row), _const_spec(w.shape)]
        args += [a, w]
    in_specs += [_const_spec((1, d)), _const_spec(wg.shape), _const_spec(wu.shape),
                 _const_spec(wd.shape), pl.BlockSpec((tm, p.shape[1]), row), _const_spec((1, d)),
                 _const_spec(wgate.shape), _const_spec(wproj.shape), _const_spec((1, d))]
    args += [g, wg, wu, wd, p, pg, wgate, wproj, fg]
    return pl.pallas_call(
        functools.partial(_post_body, n_pairs=len(pairs), final=final),
        grid=(t // tm,),
        in_specs=in_specs,
        out_specs=pl.BlockSpec((tm, d), row),
        out_shape=jax.ShapeDtypeStruct((t, d), F32),
        compiler_params=_cparams(1),
        name="post",
    )(*args)


def _slab(ref_or_val, i):
    return ref_or_val[:, LANES * i:LANES * (i + 1)]


def _dot_nt(a, b):
    return lax.dot_general(a, b, (((1,), (1,)), ((), ())), preferred_element_type=F32)


def _values_t(wvt, h, dv):
    vt = _dot_nt(wvt, h)
    row = lax.broadcasted_iota(jnp.int32, vt.shape, 0)
    return jnp.where(row % (dv + ONES_PAD) == dv, 1.0, vt).astype(BF16)


def _pre_even_body(x_ref, fg_ref, fwg_ref, fwu_ref, fwd_ref,
                   g_ref, w1_ref, wvat_ref, cosa_ref, sina_ref, cosb_ref, sinb_ref,
                   gq_ref, gqp_ref, gk_ref, gkp_ref, bqn_ref, wuq_ref, bkvn_ref, wukv_ref,
                   wvbt_ref, xo_ref, qa_ref, ka_ref, vat_ref, qb_ref, kb_ref, vbt_ref):
    x = _half_swiglu(x_ref[...], fg_ref, fwg_ref, fwu_ref, fwd_ref)
    xo_ref[...] = x
    h = _rms(x, g_ref[...]).astype(BF16)
    z = jnp.dot(h, w1_ref[...], preferred_element_type=F32)
    vat_ref[...] = _values_t(wvat_ref[...], h, A_HEAD_DIM)
    cosa, sina = cosa_ref[...], sina_ref[...]
    cosb, sinb = cosb_ref[...], sinb_ref[...]

    scale_a = A_HEAD_DIM ** -0.5 * LOG2E
    cq, sq = cosa * (gq_ref[...] * scale_a), sina * (gqp_ref[...] * scale_a)
    ck, sk = cosa * gk_ref[...], sina * gkp_ref[...]
    low = lax.broadcasted_iota(jnp.int32, (x_ref.shape[0], LANES), 1) < A_HEAD_DIM

    def half_rsqrt(zz):
        sq2 = zz * zz
        lo = jnp.sum(jnp.where(low, sq2, 0.0), axis=-1, keepdims=True)
        hi = jnp.sum(jnp.where(low, 0.0, sq2), axis=-1, keepdims=True)
        return lax.rsqrt(jnp.where(low, lo, hi) * (1.0 / A_HEAD_DIM) + EPS)

    n_q, n_k = A_HEADS // 2, A_KV_HEADS
    for i in range(n_q):
        zz, zr = _slab(z, i), _slab(z, n_q + i)
        qa_ref[:, LANES * i:LANES * (i + 1)] = ((zz * cq + zr * sq) * half_rsqrt(zz)).astype(BF16)
    for i in range(n_k):
        zz, zr = _slab(z, 2 * n_q + i), _slab(z, 2 * n_q + n_k + i)
        ka_ref[:, LANES * i:LANES * (i + 1)] = ((zz * ck + zr * sk) * half_rsqrt(zz)).astype(BF16)

    c0 = LANES * (2 * n_q + 2 * n_k)
    scale_b = (B_NOPE + B_ROPE) ** -0.5 * LOG2E
    cqn = _rms(z[:, c0:c0 + B_Q_RANK], bqn_ref[...]).astype(BF16)
    zq = jnp.dot(cqn, wuq_ref[...], preferred_element_type=F32)
    for i in range(B_HEADS):
        qb_ref[:, LANES * i:LANES * (i + 1)] = (
            (_slab(zq, i) * cosb + _slab(zq, B_HEADS + i) * sinb) * scale_b).astype(BF16)
    c1 = c0 + B_Q_RANK
    ckvn = _rms(z[:, c1:c1 + B_KV_RANK], bkvn_ref[...]).astype(BF16)
    zk = jnp.dot(ckvn, wukv_ref[...], preferred_element_type=F32)
    vbt_ref[...] = _values_t(wvbt_ref[...], ckvn, B_VDIM)
    c2 = c1 + B_KV_RANK
    kr = z[:, c2:c2 + LANES] * cosb + z[:, c2 + LANES:c2 + 2 * LANES] * sinb
    for i in range(B_HEADS):
        kb_ref[:, LANES * i:LANES * (i + 1)] = (_slab(zk, i) + kr).astype(BF16)


def _pre_even(x, ffn, g, w1, wvat, tabs, gq, gqp, gk, gkp, bqn, wuq, bkvn, wukv, wvbt, seq, tm):
    t, d = x.shape
    nt = seq // tm
    tab_spec = pl.BlockSpec((tm, LANES), lambda i: (i % nt, 0))
    row = lambda i: (i, 0)
    col = lambda i: (0, i)
    outs = ((d, row, F32),
            (A_HEADS // 2 * LANES, row, BF16), (A_KV_HEADS * LANES, row, BF16),
            (wvat.shape[0], col, BF16),
            (B_HEADS * LANES, row, BF16), (B_HEADS * LANES, row, BF16),
            (wvbt.shape[0], col, BF16))
    return pl.pallas_call(
        _pre_even_body,
        grid=(t // tm,),
        in_specs=[pl.BlockSpec((tm, d), row)] + _ffn_specs(ffn)
                 + [_const_spec((1, d)), _const_spec(w1.shape), _const_spec(wvat.shape),
                    tab_spec, tab_spec, tab_spec, tab_spec,
                    _const_spec((1, LANES)), _const_spec((1, LANES)),
                    _const_spec((1, LANES)), _const_spec((1, LANES)),
                    _const_spec((1, B_Q_RANK)), _const_spec(wuq.shape),
                    _const_spec((1, B_KV_RANK)), _const_spec(wukv.shape),
                    _const_spec(wvbt.shape)],
        out_specs=[pl.BlockSpec((tm, n) if m is row else (n, tm), m) for n, m, _ in outs],
        out_shape=[jax.ShapeDtypeStruct((t, n) if m is row else (n, t), dt)
                   for n, m, dt in outs],
        compiler_params=_cparams(1),
        name="pre_even",
    )(x, *ffn, g, w1, wvat, *tabs, gq, gqp, gk, gkp, bqn, wuq, bkvn, wukv, wvbt)


def _pre_odd_body(x_ref, fg_ref, fwg_ref, fwu_ref, fwd_ref, g_ref, w_ref, wvt_ref,
                  xo_ref, q_ref, k_ref, vt_ref):
    x = _half_swiglu(x_ref[...], fg_ref, fwg_ref, fwu_ref, fwd_ref)
    xo_ref[...] = x
    h = _rms(x, g_ref[...]).astype(BF16)
    z = jnp.dot(h, w_ref[...], preferred_element_type=F32)
    nq = C_HEADS * LANES
    q_ref[...] = (z[:, :nq] * (C_HEAD_DIM ** -0.5 * LOG2E)).astype(BF16)
    k_ref[...] = z[:, nq:].astype(BF16)
    vt_ref[...] = _values_t(wvt_ref[...], h, 2 * C_HEAD_DIM)


def _pre_odd(x, ffn, g, w, wvt, tm):
    t, d = x.shape
    row = lambda i: (i, 0)
    return pl.pallas_call(
        _pre_odd_body,
        grid=(t // tm,),
        in_specs=[pl.BlockSpec((tm, d), row)] + _ffn_specs(ffn)
                 + [_const_spec((1, d)), _const_spec(w.shape), _const_spec(wvt.shape)],
        out_specs=[pl.BlockSpec((tm, d), row),
                   pl.BlockSpec((tm, C_HEADS * LANES), row),
                   pl.BlockSpec((tm, C_HEADS * LANES), row),
                   pl.BlockSpec((wvt.shape[0], tm), lambda i: (0, i))],
        out_shape=[jax.ShapeDtypeStruct((t, d), F32),
                   jax.ShapeDtypeStruct((t, C_HEADS * LANES), BF16),
                   jax.ShapeDtypeStruct((t, C_HEADS * LANES), BF16),
                   jax.ShapeDtypeStruct((wvt.shape[0], t), BF16)],
        compiler_params=_cparams(1),
        name="pre_odd",
    )(x, *ffn, g, w, wvt)


def _attn_body(*refs, groups, packed, tq, tk, nt, seq, diff, lambda_init):
    if diff:
        (slopes_ref, q_ref, k_ref, vt_ref, lq1_ref, lk1_ref, lq2_ref, lk2_ref, subg_ref, o_ref,
         qs_ref, s_ref, p_ref, alpha_ref, m_ref, acc_ref, bias_ref) = refs
    else:
        q_ref, k_ref, vt_ref, o_ref, qs_ref, s_ref, p_ref, alpha_ref, m_ref, acc_ref = refs
    dv = vt_ref.shape[0] - ONES_PAD
    lane = lax.broadcasted_iota(jnp.int32, (tq, LANES), 1)
    for t in range(nt):
        for g in range(groups):
            if packed:
                qg = q_ref[0, t * tq:(t + 1) * tq, LANES * (g // 2):LANES * (g // 2 + 1)]
                qg = jnp.where((lane >= LANES // 2) == (g % 2 == 1), qg, jnp.zeros_like(qg))
            else:
                qg = q_ref[0, t * tq:(t + 1) * tq, LANES * g:LANES * (g + 1)]
            qs_ref[t, g * tq:(g + 1) * tq, :] = qg
    m_ref[...] = jnp.full(m_ref.shape, NEG_BIG, F32)
    acc_ref[...] = jnp.zeros(acc_ref.shape, F32)
    if diff:
        slope = slopes_ref[pl.program_id(1)] * LOG2E
        q_base = pl.program_id(2) * (nt * tq)
        rel = (lax.broadcasted_iota(jnp.int32, (tk, tq), 1)
               - lax.broadcasted_iota(jnp.int32, (tk, tq), 0)).astype(F32)
        bias_ref[0] = slope * rel
        bias_ref[1] = -slope * jnp.abs(rel)
        bias_ref[2] = -slope * rel

    def scores(item, slot):
        t, c = item
        s = _dot_nt(k_ref[0, c * tk:(c + 1) * tk, :], qs_ref[t])
        if diff:
            tile = bias_ref[jnp.sign(q_base + (t * tq - c * tk)) + 1]
            s = s + jnp.concatenate([tile] * groups, axis=1)
        s_ref[slot] = s

    def softmax(item, s_slot, slot):
        t, c = item
        s = s_ref[s_slot]
        m = m_ref[t]
        if diff:
            delta = q_base + (t * tq - c * tk)
            off = slope * jnp.abs(delta).astype(F32)
            m_new = jnp.maximum(m, jnp.max(s, axis=0, keepdims=True) - off)
            shift = m_new + off
        else:
            m_new = jnp.maximum(m, jnp.max(s, axis=0, keepdims=True))
            shift = m_new
        m_ref[t] = m_new
        alpha_ref[slot] = jnp.exp2(m - m_new)
        p_ref[slot] = jnp.exp2(s - shift).astype(BF16)

    def values(item, slot):
        t, c = item
        pv = jnp.dot(vt_ref[:, c * tk:(c + 1) * tk], p_ref[slot], preferred_element_type=F32)
        acc_ref[t] = alpha_ref[slot] * acc_ref[t] + pv

    def finalize(t):
        ot = acc_ref[t, :dv, :] / acc_ref[t, dv:dv + 1, :]
        rows = slice(t * tq, (t + 1) * tq)
        if diff:
            lam = (jnp.exp(jnp.sum(lq1_ref[...] * lk1_ref[...], keepdims=True))
                   - jnp.exp(jnp.sum(lq2_ref[...] * lk2_ref[...], keepdims=True)) + lambda_init)
            od = ot[:, :tq] - lam * ot[:, tq:]
            r = lax.rsqrt(jnp.mean(od * od, axis=0, keepdims=True) + EPS)
            od = od * r * (subg_ref[...] * (1.0 - lambda_init))
            o_ref[0, rows, :] = od.T.astype(BF16)
        elif packed:
            for g in range(0, groups, 2):
                og = jnp.concatenate([ot[:, g * tq:(g + 1) * tq],
                                      ot[:, (g + 1) * tq:(g + 2) * tq]], axis=0)
                o_ref[0, rows, LANES * (g // 2):LANES * (g // 2 + 1)] = og.T.astype(BF16)
        else:
            pad = jnp.zeros((LANES - dv, tq), F32)
            for g in range(groups):
                og = jnp.concatenate([ot[:, g * tq:(g + 1) * tq], pad], axis=0)
                o_ref[0, rows, LANES * g:LANES * (g + 1)] = og.T.astype(BF16)

    n_chunks = seq // tk
    items = [(t, c) for t in range(nt) for c in range(n_chunks)]
    ahead = s_ref.shape[0]
    for u in range(min(ahead, len(items))):
        scores(items[u], u % ahead)
    softmax(items[0], 0, 0)
    for u, item in enumerate(items):
        if u + ahead < len(items):
            scores(items[u + ahead], u % ahead)
        values(item, u % 2)
        if item[1] == n_chunks - 1:
            finalize(item[0])
        if u + 1 < len(items):
            softmax(items[u + 1], (u + 1) % ahead, (u + 1) % 2)


def _attention(q, k, vt, groups, tiles, packed=False, diff_args=None, lambda_init=0.0):
    b, s, _ = q.shape
    nkv = k.shape[2] // LANES
    q_slabs = groups // 2 if packed else groups
    tq, tk, items = tiles
    tq, tk = min(tq, s), min(tk, s)
    nt = max(1, min(s // tq, items // (s // tk)))
    assert s % tk == 0 and s // tk >= 2 and s % (nt * tq) == 0, (s, tiles)
    cols = groups * tq
    vrows = vt.shape[0] // nkv
    scratch = [pltpu.VMEM((nt, cols, LANES), BF16),
               pltpu.VMEM((2, tk, cols), F32),
               pltpu.VMEM((2, tk, cols), BF16),
               pltpu.VMEM((2, 1, cols), F32),
               pltpu.VMEM((nt, 1, cols), F32),
               pltpu.VMEM((nt, vrows, cols), F32)]
    diff = diff_args is not None
    q_spec = pl.BlockSpec((1, nt * tq, q_slabs * LANES), lambda bi, h, i: (bi, i, h))
    k_spec = pl.BlockSpec((1, s, LANES), lambda bi, h, i: (bi, 0, h))
    vt_spec = pl.BlockSpec((vrows, s), lambda bi, h, i: (h, bi))
    in_specs = [q_spec, k_spec, vt_spec]
    args = [q, k, vt]
    out_w = 1 if diff else q_slabs
    if diff:
        assert tq == tk, (tq, tk)
        scratch.append(pltpu.VMEM((3, tk, tq), F32))
        slopes, lq1, lk1, lq2, lk2, subg = diff_args
        in_specs = ([pl.BlockSpec(memory_space=pltpu.SMEM)] + in_specs
                    + [_const_spec((1, C_HEAD_DIM))] * 4 + [_const_spec((LANES, 1))])
        args = [slopes] + args + [lq1, lk1, lq2, lk2, subg]
    return pl.pallas_call(
        functools.partial(_attn_body, groups=groups, packed=packed, tq=tq, tk=tk, nt=nt, seq=s,
                          diff=diff, lambda_init=lambda_init),
        grid=(b, nkv, s // (nt * tq)),
        in_specs=in_specs,
        out_specs=pl.BlockSpec((1, nt * tq, out_w * LANES), lambda bi, h, i: (bi, i, h)),
        out_shape=jax.ShapeDtypeStruct((b, s, nkv * out_w * LANES), BF16),
        scratch_shapes=scratch,
        compiler_params=_cparams(3),
        name="attn_diff" if diff else f"attn_g{groups}",
    )(*args)


def _pad_slabs(w, n, width, offset=0):
    kdim = w.shape[0]
    w = w.reshape(kdim, n, width)
    w = jnp.pad(w, ((0, 0), (0, 0), (offset, LANES - width - offset)))
    return w.reshape(kdim, n * LANES)


def _rot_cols(w, n, perm, sign):
    kdim = w.shape[0]
    width = perm.shape[0]
    w = w.reshape(kdim, n, width)[:, :, perm] * sign
    return w.reshape(kdim, n * width)


_HALF = B_ROPE // 2
_PERM32 = np.concatenate([np.arange(_HALF, 2 * _HALF), np.arange(_HALF)])
_SIGN32 = np.concatenate([-np.ones(_HALF), np.ones(_HALF)]).astype(np.float32)
_PERM64 = np.concatenate([_PERM32, _PERM32 + 2 * _HALF])
_SIGN64 = np.concatenate([_SIGN32, _SIGN32])


def _rope_tables(seq):
    t = jnp.arange(seq, dtype=jnp.int32)
    inv = ROPE_BASE ** (-jnp.arange(_HALF, dtype=F32) * (2.0 / (2 * _HALF)))

    def cs(pos):
        ang = pos.astype(F32)[:, None] * inv[None, :]
        return jnp.cos(ang), jnp.sin(ang)

    cr, sr = cs(t // GRID_W)
    cc, sc = cs(t % GRID_W)
    ct, st = cs(t)
    cosa = jnp.concatenate([cr, cr, cc, cc] * (LANES // A_HEAD_DIM), axis=1)
    sina = jnp.concatenate([sr, sr, sc, sc] * (LANES // A_HEAD_DIM), axis=1)
    zb = jnp.zeros((seq, LANES - B_NOPE - B_ROPE), F32)
    cosb = jnp.concatenate([jnp.ones((seq, B_NOPE), F32), ct, ct, zb], axis=1)
    sinb = jnp.concatenate([jnp.zeros((seq, B_NOPE), F32), st, st, zb], axis=1)
    return cosa, sina, cosb, sinb


def _vt_weights(w, dv):
    kdim, n, _ = w.shape
    w = jnp.pad(w, ((0, 0), (0, 0), (0, ONES_PAD)))
    return w.reshape(kdim, n * (dv + ONES_PAD)).T.astype(BF16)


def _lane_row(g, perm=None):
    g = g.astype(F32)
    if perm is not None:
        g = g[perm]
    return jnp.tile(g, LANES // g.shape[0]).reshape(1, LANES)


def _prep_even(w_in, a_qn, a_kn, b_qn, b_wuq, b_kvn, b_wukv, w_out):
    na, nk = A_HEADS * A_HEAD_DIM, A_KV_HEADS * A_HEAD_DIM
    wq, wk, wv = w_in[:, :na], w_in[:, na:na + nk], w_in[:, na + nk:na + 2 * nk]
    c = na + 2 * nk
    wcq, wckv = w_in[:, c:c + B_Q_RANK], w_in[:, c + B_Q_RANK:c + B_Q_RANK + B_KV_RANK]
    wkr = w_in[:, c + B_Q_RANK + B_KV_RANK:]
    wk_rot = _rot_cols(wk, A_KV_HEADS, _PERM64, _SIGN64)
    twice = lambda w: jnp.tile(w.reshape(D_MODEL, A_KV_HEADS, 1, A_HEAD_DIM),
                               (1, 1, 2, 1)).reshape(D_MODEL, 2 * nk)
    w1 = jnp.concatenate([
        wq, _rot_cols(wq, A_HEADS, _PERM64, _SIGN64),
        twice(wk), twice(wk_rot),
        wcq, wckv,
        _pad_slabs(wkr, 1, B_ROPE, offset=B_NOPE),
        _pad_slabs(_rot_cols(wkr, 1, _PERM32, _SIGN32), 1, B_ROPE, offset=B_NOPE),
    ], axis=1).astype(BF16)
    dq = B_NOPE + B_ROPE
    uq = b_wuq.reshape(B_Q_RANK, B_HEADS, dq)
    uq_rot = uq[:, :, B_NOPE:][:, :, _PERM32] * _SIGN32
    wuq = jnp.concatenate([
        _pad_slabs(uq.reshape(B_Q_RANK, B_HEADS * dq), B_HEADS, dq),
        _pad_slabs(uq_rot.reshape(B_Q_RANK, B_HEADS * B_ROPE), B_HEADS, B_ROPE, offset=B_NOPE),
    ], axis=1).astype(BF16)
    ukv = b_wukv.reshape(B_KV_RANK, B_HEADS, B_NOPE + B_VDIM)
    wukv = _pad_slabs(ukv[:, :, :B_NOPE].reshape(B_KV_RANK, -1), B_HEADS, B_NOPE).astype(BF16)
    wvbt = _vt_weights(ukv[:, :, B_NOPE:], B_VDIM)
    wvat = _vt_weights(wv.reshape(D_MODEL, A_KV_HEADS, A_HEAD_DIM), A_HEAD_DIM)
    wo_a = w_out[:na]
    wo_b = jnp.pad(w_out[na:].reshape(B_HEADS, B_VDIM, D_MODEL),
                   ((0, 0), (0, LANES - B_VDIM), (0, 0))).reshape(B_HEADS * LANES, D_MODEL)
    return dict(
        w1=w1, wvat=wvat, wuq=wuq, wukv=wukv, wvbt=wvbt, wo_a=wo_a.astype(BF16), wo_b=wo_b.astype(BF16),
        gq=_lane_row(a_qn), gqp=_lane_row(a_qn, _PERM64),
        gk=_lane_row(a_kn), gkp=_lane_row(a_kn, _PERM64),
        bqn=b_qn.reshape(1, -1), bkvn=b_kvn.reshape(1, -1))


def _prep_odd(w_in, w_out):
    n = C_HEADS * 2 * C_HEAD_DIM
    w = w_in[:, :2 * n].astype(BF16)
    wvt = _vt_weights(w_in[:, 2 * n:].reshape(D_MODEL, C_HEADS, 2 * C_HEAD_DIM), 2 * C_HEAD_DIM)
    return dict(w=w, wvt=wvt, wo=w_out.astype(BF16))


def _alibi_slopes():
    return 2.0 ** (-8.0 * jnp.arange(1, C_HEADS + 1, dtype=F32) / C_HEADS)


def _trunk(x, p, lw, final_norm, tm=512,
           tiles_a=(128, 256, 64), tiles_b=(512, 256, 64), tiles_c=(256, 256, 64)):
    b, s, d = x.shape
    t = b * s
    tm_ffn = min(tm, s)
    x = x.reshape(t, d)
    tabs = _rope_tables(s)
    slopes = _alibi_slopes()
    fg = final_norm.reshape(1, d)
    for i in range(DEPTH):
        w = lw[i]
        ffn1 = (w["ffn1_norm"], w["ffn1_wg"], w["ffn1_wu"], w["ffn1_wd"])
        if i % 2 == 0:
            m = w["mix"]
            x, qa, ka, vat, qb, kb, vbt = _pre_even(
                x, ffn1, w["mix_norm"], m["w1"], m["wvat"], tabs, m["gq"], m["gqp"], m["gk"],
                m["gkp"], m["bqn"], m["wuq"], m["bkvn"], m["wukv"], m["wvbt"], s, tm_ffn)
            r3 = lambda a: a.reshape(b, s, a.shape[1])
            oa = _attention(r3(qa), r3(ka), vat, A_HEADS // A_KV_HEADS, tiles_a, packed=True)
            ob = _attention(r3(qb), r3(kb), vbt, 1, tiles_b)
            mixed = [(oa.reshape(t, -1), m["wo_a"]), (ob.reshape(t, -1), m["wo_b"])]
        else:
            m = w["mix"]
            lambda_init = 0.8 - 0.6 * math.exp(-0.3 * i)
            x, q, k, vt = _pre_odd(x, ffn1, w["mix_norm"], m["w"], m["wvt"], tm_ffn)
            r3 = lambda a: a.reshape(b, s, a.shape[1])
            oc = _attention(r3(q), r3(k), vt, 2, tiles_c, packed=True,
                            diff_args=(slopes, w["lq1"], w["lk1"], w["lq2"], w["lk2"], w["subg"]),
                            lambda_init=lambda_init)
            mixed = [(oc.reshape(t, -1), m["wo"])]
        x = _post(x, mixed, w["ffn2_norm"], w["ffn2_wg"], w["ffn2_wu"], w["ffn2_wd"],
                  p[i].reshape(t, -1), w["ple_norm"], w["ple_w_gate"], w["ple_w_proj"], fg,
                  i == DEPTH - 1, tm_ffn)
    return x.reshape(b, s, d)


def _layer_weights(ffn1_norm, ffn1_wg, ffn1_wu, ffn1_wd, mix_norm, ab_w_in, a_q_norm, a_k_norm,
                   b_q_norm, b_w_uq, b_kv_norm, b_w_ukv, ab_w_out, c_w_in, c_lambda_q1,
                   c_lambda_k1, c_lambda_q2, c_lambda_k2, c_sub_norm, c_w_out, ffn2_norm, ffn2_wg,
                   ffn2_wu, ffn2_wd, ple_norm, ple_w_gate, ple_w_proj):
    layers = []
    for i in range(DEPTH):
        w = dict(
            ffn1_norm=ffn1_norm[i].reshape(1, -1), ffn1_wg=ffn1_wg[i].astype(BF16),
            ffn1_wu=ffn1_wu[i].astype(BF16), ffn1_wd=ffn1_wd[i].astype(BF16),
            mix_norm=mix_norm[i].reshape(1, -1),
            ffn2_norm=ffn2_norm[i].reshape(1, -1), ffn2_wg=ffn2_wg[i].astype(BF16),
            ffn2_wu=ffn2_wu[i].astype(BF16), ffn2_wd=ffn2_wd[i].astype(BF16),
            ple_norm=ple_norm[i].reshape(1, -1), ple_w_gate=ple_w_gate[i].astype(BF16),
            ple_w_proj=ple_w_proj[i].astype(BF16))
        j = i // 2
        if i % 2 == 0:
            w["mix"] = _prep_even(ab_w_in[j], a_q_norm[j], a_k_norm[j], b_q_norm[j], b_w_uq[j],
                                  b_kv_norm[j], b_w_ukv[j], ab_w_out[j])
        else:
            w["mix"] = _prep_odd(c_w_in[j], c_w_out[j])
            w["lq1"] = c_lambda_q1[j].reshape(1, -1)
            w["lk1"] = c_lambda_k1[j].reshape(1, -1)
            w["lq2"] = c_lambda_q2[j].reshape(1, -1)
            w["lk2"] = c_lambda_k2[j].reshape(1, -1)
            w["subg"] = c_sub_norm[j].reshape(-1, 1)
        layers.append(w)
    return layers


def kernel(x_prompt, x_sample, p_prompt, p_sample, ffn1_norm, ffn1_wg, ffn1_wu, ffn1_wd, mix_norm, ab_w_in, a_q_norm, a_k_norm, b_q_norm, b_w_uq, b_kv_norm, b_w_ukv, ab_w_out, c_w_in, c_lambda_q1, c_lambda_k1, c_lambda_q2, c_lambda_k2, c_sub_norm, c_w_out, ffn2_norm, ffn2_wg, ffn2_wu, ffn2_wd, ple_norm, ple_w_gate, ple_w_proj, final_norm):
    lw = _layer_weights(ffn1_norm, ffn1_wg, ffn1_wu, ffn1_wd, mix_norm, ab_w_in, a_q_norm,
                        a_k_norm, b_q_norm, b_w_uq, b_kv_norm, b_w_ukv, ab_w_out, c_w_in,
                        c_lambda_q1, c_lambda_k1, c_lambda_q2, c_lambda_k2, c_sub_norm, c_w_out,
                        ffn2_norm, ffn2_wg, ffn2_wu, ffn2_wd, ple_norm, ple_w_gate, ple_w_proj)
    y_prompt = _trunk(x_prompt, p_prompt, lw, final_norm)
    y_sample = _trunk(x_sample, p_sample, lw, final_norm)
    return (y_prompt, y_sample)
```

```python
import functools
import math

import numpy as np
import jax
import jax.numpy as jnp
from jax import lax
from jax.experimental import pallas as pl
from jax.experimental.pallas import tpu as pltpu

D_MODEL = 1024
DEPTH = 4
GRID_W = 64
EPS = 1e-6
D_FF = 2816
D_PLE = 256
ROPE_BASE = 10000.0
A_HEADS = 8
A_KV_HEADS = 2
A_HEAD_DIM = 64
B_HEADS = 8
B_NOPE = 64
B_ROPE = 32
B_VDIM = 64
B_Q_RANK = 256
B_KV_RANK = 128
C_HEADS = 8
C_HEAD_DIM = 64

LANES = 128
VMEM_LIMIT = 56 * 1024 * 1024
NEG_BIG = -1e30
LOG2E = math.log2(math.e)
ONES_PAD = 16

F32 = jnp.float32
BF16 = jnp.bfloat16


def _cparams(n_axes):
    return pltpu.CompilerParams(
        dimension_semantics=("parallel",) * n_axes, vmem_limit_bytes=VMEM_LIMIT)


def _const_spec(shape):
    nd = len(shape)
    return pl.BlockSpec(shape, lambda *_: (0,) * nd, pipeline_mode=pl.Buffered(1))


def _rms(x, g):
    return x * lax.rsqrt(jnp.mean(x * x, axis=-1, keepdims=True) + EPS) * g


def _sigmoid(x):
    return 1.0 / (1.0 + jnp.exp(-x))


def _half_swiglu(x, g_ref, wg_ref, wu_ref, wd_ref):
    xn = _rms(x, g_ref[...]).astype(BF16)
    g = jnp.dot(xn, wg_ref[...], preferred_element_type=F32)
    u = jnp.dot(xn, wu_ref[...], preferred_element_type=F32)
    a = (g * _sigmoid(g) * u).astype(BF16)
    return x + 0.5 * jnp.dot(a, wd_ref[...], preferred_element_type=F32)


def _ffn_specs(ffn):
    return [_const_spec(a.shape) for a in ffn]


def _post_body(*refs, n_pairs, final):
    x_ref = refs[0]
    pairs = refs[1:1 + 2 * n_pairs]
    (g_ref, wg_ref, wu_ref, wd_ref, p_ref, pg_ref, wgate_ref, wproj_ref, fg_ref,
     o_ref) = refs[1 + 2 * n_pairs:]
    x = x_ref[...]
    for a_ref, w_ref in zip(pairs[0::2], pairs[1::2]):
        x = x + jnp.dot(a_ref[...], w_ref[...], preferred_element_type=F32)
    x = _half_swiglu(x, g_ref, wg_ref, wu_ref, wd_ref)
    xn = _rms(x, pg_ref[...]).astype(BF16)
    gate = _sigmoid(jnp.dot(xn, wgate_ref[...], preferred_element_type=F32))
    proj = jnp.dot(p_ref[...].astype(BF16), wproj_ref[...], preferred_element_type=F32)
    y = x + gate * proj
    if final:
        y = _rms(y, fg_ref[...])
    o_ref[...] = y


def _post(x, pairs, g, wg, wu, wd, p, layer, pg, wgate, wproj, fg, final, tm):
    t, d = x.shape
    row = lambda i: (i, 0)
    p_row = lambda i: (layer * (t // tm) + i, 0)
    in_specs = [pl.BlockSpec((tm, d), row)]
    args = [x]
    for a, w in pairs:
        in_specs += [pl.BlockSpec((tm, a.shape[1]), row), _const_spec(w.shape)]
        args += [a, w]
    in_specs += [_const_spec((1, d)), _const_spec(wg.shape), _const_spec(wu.shape),
                 _const_spec(wd.shape), pl.BlockSpec((tm, p.shape[1]), p_row), _const_spec((1, d)),
                 _const_spec(wgate.shape), _const_spec(wproj.shape), _const_spec((1, d))]
    args += [g, wg, wu, wd, p, pg, wgate, wproj, fg]
    return pl.pallas_call(
        functools.partial(_post_body, n_pairs=len(pairs), final=final),
        grid=(t // tm,),
        in_specs=in_specs,
        out_specs=pl.BlockSpec((tm, d), row),
        out_shape=jax.ShapeDtypeStruct((t, d), F32),
        compiler_params=_cparams(1),
        name="post",
    )(*args)


def _slab(ref_or_val, i):
    return ref_or_val[:, LANES * i:LANES * (i + 1)]


def _dot_nt(a, b):
    return lax.dot_general(a, b, (((1,), (1,)), ((), ())), preferred_element_type=F32)


def _values_t(wvt, h, dv):
    vt = _dot_nt(wvt, h)
    row = lax.broadcasted_iota(jnp.int32, vt.shape, 0)
    return jnp.where(row % (dv + ONES_PAD) == dv, 1.0, vt).astype(BF16)


def _pre_even_body(x_ref, fg_ref, fwg_ref, fwu_ref, fwd_ref,
                   g_ref, w1_ref, wvat_ref, cosa_ref, sina_ref, cosb_ref, sinb_ref,
                   gq_ref, gqp_ref, gk_ref, gkp_ref, bqn_ref, wuq_ref, bkvn_ref, wukv_ref,
                   wvbt_ref, xo_ref, qa_ref, ka_ref, vat_ref, qb_ref, kb_ref, vbt_ref):
    x = _half_swiglu(x_ref[...], fg_ref, fwg_ref, fwu_ref, fwd_ref)
    xo_ref[...] = x
    h = _rms(x, g_ref[...]).astype(BF16)
    z = jnp.dot(h, w1_ref[...], preferred_element_type=F32)
    vat_ref[...] = _values_t(wvat_ref[...], h, A_HEAD_DIM)
    cosa, sina = cosa_ref[...], sina_ref[...]
    cosb, sinb = cosb_ref[...], sinb_ref[...]

    scale_a = A_HEAD_DIM ** -0.5 * LOG2E
    cq, sq = cosa * (gq_ref[...] * scale_a), sina * (gqp_ref[...] * scale_a)
    ck, sk = cosa * gk_ref[...], sina * gkp_ref[...]
    low = lax.broadcasted_iota(jnp.int32, (x_ref.shape[0], LANES), 1) < A_HEAD_DIM

    def half_rsqrt(zz):
        sq2 = zz * zz
        lo = jnp.sum(jnp.where(low, sq2, 0.0), axis=-1, keepdims=True)
        hi = jnp.sum(jnp.where(low, 0.0, sq2), axis=-1, keepdims=True)
        return lax.rsqrt(jnp.where(low, lo, hi) * (1.0 / A_HEAD_DIM) + EPS)

    n_q, n_k = A_HEADS // 2, A_KV_HEADS
    for i in range(n_q):
        zz, zr = _slab(z, i), _slab(z, n_q + i)
        qa_ref[:, LANES * i:LANES * (i + 1)] = ((zz * cq + zr * sq) * half_rsqrt(zz)).astype(BF16)
    for i in range(n_k):
        zz, zr = _slab(z, 2 * n_q + i), _slab(z, 2 * n_q + n_k + i)
        ka_ref[:, LANES * i:LANES * (i + 1)] = ((zz * ck + zr * sk) * half_rsqrt(zz)).astype(BF16)

    c0 = LANES * (2 * n_q + 2 * n_k)
    scale_b = (B_NOPE + B_ROPE) ** -0.5 * LOG2E
    cqn = _rms(z[:, c0:c0 + B_Q_RANK], bqn_ref[...]).astype(BF16)
    zq = jnp.dot(cqn, wuq_ref[...], preferred_element_type=F32)
    for i in range(B_HEADS):
        qb_ref[:, LANES * i:LANES * (i + 1)] = (
            (_slab(zq, i) * cosb + _slab(zq, B_HEADS + i) * sinb) * scale_b).astype(BF16)
    c1 = c0 + B_Q_RANK
    ckvn = _rms(z[:, c1:c1 + B_KV_RANK], bkvn_ref[...]).astype(BF16)
    zk = jnp.dot(ckvn, wukv_ref[...], preferred_element_type=F32)
    vbt_ref[...] = _values_t(wvbt_ref[...], ckvn, B_VDIM)
    c2 = c1 + B_KV_RANK
    kr = z[:, c2:c2 + LANES] * cosb + z[:, c2 + LANES:c2 + 2 * LANES] * sinb
    for i in range(B_HEADS):
        kb_ref[:, LANES * i:LANES * (i + 1)] = (_slab(zk, i) + kr).astype(BF16)


def _pre_even(x, ffn, g, w1, wvat, tabs, gq, gqp, gk, gkp, bqn, wuq, bkvn, wukv, wvbt, seq, tm):
    t, d = x.shape
    nt = seq // tm
    tab_spec = pl.BlockSpec((tm, LANES), lambda i: (i % nt, 0))
    row = lambda i: (i, 0)
    col = lambda i: (0, i)
    outs = ((d, row, F32),
            (A_HEADS // 2 * LANES, row, BF16), (A_KV_HEADS * LANES, row, BF16),
            (wvat.shape[0], col, BF16),
            (B_HEADS * LANES, row, BF16), (B_HEADS * LANES, row, BF16),
            (wvbt.shape[0], col, BF16))
    return pl.pallas_call(
        _pre_even_body,
        grid=(t // tm,),
        in_specs=[pl.BlockSpec((tm, d), row)] + _ffn_specs(ffn)
                 + [_const_spec((1, d)), _const_spec(w1.shape), _const_spec(wvat.shape),
                    tab_spec, tab_spec, tab_spec, tab_spec,
                    _const_spec((1, LANES)), _const_spec((1, LANES)),
                    _const_spec((1, LANES)), _const_spec((1, LANES)),
                    _const_spec((1, B_Q_RANK)), _const_spec(wuq.shape),
                    _const_spec((1, B_KV_RANK)), _const_spec(wukv.shape),
                    _const_spec(wvbt.shape)],
        out_specs=[pl.BlockSpec((tm, n) if m is row else (n, tm), m) for n, m, _ in outs],
        out_shape=[jax.ShapeDtypeStruct((t, n) if m is row else (n, t), dt)
                   for n, m, dt in outs],
        compiler_params=_cparams(1),
        name="pre_even",
    )(x, *ffn, g, w1, wvat, *tabs, gq, gqp, gk, gkp, bqn, wuq, bkvn, wukv, wvbt)


def _pre_odd_body(x_ref, fg_ref, fwg_ref, fwu_ref, fwd_ref, g_ref, w_ref, wvt_ref,
                  xo_ref, q_ref, k_ref, vt_ref):
    x = _half_swiglu(x_ref[...], fg_ref, fwg_ref, fwu_ref, fwd_ref)
    xo_ref[...] = x
    h = _rms(x, g_ref[...]).astype(BF16)
    z = jnp.dot(h, w_ref[...], preferred_element_type=F32)
    nq = C_HEADS * LANES
    q_ref[...] = (z[:, :nq] * (C_HEAD_DIM ** -0.5 * LOG2E)).astype(BF16)
    k_ref[...] = z[:, nq:].astype(BF16)
    vt_ref[...] = _values_t(wvt_ref[...], h, 2 * C_HEAD_DIM)


def _pre_odd(x, ffn, g, w, wvt, tm):
    t, d = x.shape
    row = lambda i: (i, 0)
    return pl.pallas_call(
        _pre_odd_body,
        grid=(t // tm,),
        in_specs=[pl.BlockSpec((tm, d), row)] + _ffn_specs(ffn)
                 + [_const_spec((1, d)), _const_spec(w.shape), _const_spec(wvt.shape)],
        out_specs=[pl.BlockSpec((tm, d), row),
                   pl.BlockSpec((tm, C_HEADS * LANES), row),
                   pl.BlockSpec((tm, C_HEADS * LANES), row),
                   pl.BlockSpec((wvt.shape[0], tm), lambda i: (0, i))],
        out_shape=[jax.ShapeDtypeStruct((t, d), F32),
                   jax.ShapeDtypeStruct((t, C_HEADS * LANES), BF16),
                   jax.ShapeDtypeStruct((t, C_HEADS * LANES), BF16),
                   jax.ShapeDtypeStruct((wvt.shape[0], t), BF16)],
        compiler_params=_cparams(1),
        name="pre_odd",
    )(x, *ffn, g, w, wvt)


def _attn_body(*refs, groups, packed, tq, tk, nt, seq, diff, lambda_init):
    if diff:
        (slopes_ref, q_ref, k_ref, vt_ref, lq1_ref, lk1_ref, lq2_ref, lk2_ref, subg_ref, o_ref,
         qs_ref, s_ref, p_ref, alpha_ref, m_ref, acc_ref, bias_ref) = refs
    else:
        q_ref, k_ref, vt_ref, o_ref, qs_ref, s_ref, p_ref, alpha_ref, m_ref, acc_ref = refs
    dv = vt_ref.shape[0] - ONES_PAD
    lane = lax.broadcasted_iota(jnp.int32, (tq, LANES), 1)
    for t in range(nt):
        for g in range(groups):
            if packed:
                qg = q_ref[0, t * tq:(t + 1) * tq, LANES * (g // 2):LANES * (g // 2 + 1)]
                qg = jnp.where((lane >= LANES // 2) == (g % 2 == 1), qg, jnp.zeros_like(qg))
            else:
                qg = q_ref[0, t * tq:(t + 1) * tq, LANES * g:LANES * (g + 1)]
            qs_ref[t, g * tq:(g + 1) * tq, :] = qg
    m_ref[...] = jnp.full(m_ref.shape, NEG_BIG, F32)
    acc_ref[...] = jnp.zeros(acc_ref.shape, F32)
    if diff:
        slope = slopes_ref[pl.program_id(1)] * LOG2E
        q_base = pl.program_id(2) * (nt * tq)
        rel = (lax.broadcasted_iota(jnp.int32, (tk, tq), 1)
               - lax.broadcasted_iota(jnp.int32, (tk, tq), 0)).astype(F32)
        bias_ref[0] = slope * rel
        bias_ref[1] = -slope * jnp.abs(rel)
        bias_ref[2] = -slope * rel

    def scores(item, slot):
        t, c = item
        s = _dot_nt(k_ref[0, c * tk:(c + 1) * tk, :], qs_ref[t])
        if diff:
            tile = bias_ref[jnp.sign(q_base + (t * tq - c * tk)) + 1]
            s = s + jnp.concatenate([tile] * groups, axis=1)
        s_ref[slot] = s

    def softmax(item, s_slot, slot):
        t, c = item
        s = s_ref[s_slot]
        m = m_ref[t]
        if diff:
            delta = q_base + (t * tq - c * tk)
            off = slope * jnp.abs(delta).astype(F32)
            m_new = jnp.maximum(m, jnp.max(s, axis=0, keepdims=True) - off)
            shift = m_new + off
        else:
            m_new = jnp.maximum(m, jnp.max(s, axis=0, keepdims=True))
            shift = m_new
        m_ref[t] = m_new
        alpha_ref[slot] = jnp.exp2(m - m_new)
        p_ref[slot] = jnp.exp2(s - shift).astype(BF16)

    def values(item, slot):
        t, c = item
        pv = jnp.dot(vt_ref[:, c * tk:(c + 1) * tk], p_ref[slot], preferred_element_type=F32)
        acc_ref[t] = alpha_ref[slot] * acc_ref[t] + pv

    def finalize(t):
        ot = acc_ref[t, :dv, :] / acc_ref[t, dv:dv + 1, :]
        rows = slice(t * tq, (t + 1) * tq)
        if diff:
            lam = (jnp.exp(jnp.sum(lq1_ref[...] * lk1_ref[...], keepdims=True))
                   - jnp.exp(jnp.sum(lq2_ref[...] * lk2_ref[...], keepdims=True)) + lambda_init)
            od = ot[:, :tq] - lam * ot[:, tq:]
            r = lax.rsqrt(jnp.mean(od * od, axis=0, keepdims=True) + EPS)
            od = od * r * (subg_ref[...] * (1.0 - lambda_init))
            o_ref[0, rows, :] = od.T.astype(BF16)
        elif packed:
            for g in range(0, groups, 2):
                og = jnp.concatenate([ot[:, g * tq:(g + 1) * tq],
                                      ot[:, (g + 1) * tq:(g + 2) * tq]], axis=0)
                o_ref[0, rows, LANES * (g // 2):LANES * (g // 2 + 1)] = og.T.astype(BF16)
        else:
            pad = jnp.zeros((LANES - dv, tq), F32)
            for g in range(groups):
                og = jnp.concatenate([ot[:, g * tq:(g + 1) * tq], pad], axis=0)
                o_ref[0, rows, LANES * g:LANES * (g + 1)] = og.T.astype(BF16)

    n_chunks = seq // tk
    items = [(t, c) for t in range(nt) for c in range(n_chunks)]
    ahead = s_ref.shape[0]
    for u in range(min(ahead, len(items))):
        scores(items[u], u % ahead)
    softmax(items[0], 0, 0)
    for u, item in enumerate(items):
        if u + ahead < len(items):
            scores(items[u + ahead], u % ahead)
        values(item, u % 2)
        if item[1] == n_chunks - 1:
            finalize(item[0])
        if u + 1 < len(items):
            softmax(items[u + 1], (u + 1) % ahead, (u + 1) % 2)


def _attention(q, k, vt, groups, tiles, packed=False, diff_args=None, lambda_init=0.0):
    b, s, _ = q.shape
    nkv = k.shape[2] // LANES
    q_slabs = groups // 2 if packed else groups
    tq, tk, items = tiles
    tq, tk = min(tq, s), min(tk, s)
    nt = max(1, min(s // tq, items // (s // tk)))
    assert s % tk == 0 and s // tk >= 2 and s % (nt * tq) == 0, (s, tiles)
    cols = groups * tq
    vrows = vt.shape[0] // nkv
    scratch = [pltpu.VMEM((nt, cols, LANES), BF16),
               pltpu.VMEM((2, tk, cols), F32),
               pltpu.VMEM((2, tk, cols), BF16),
               pltpu.VMEM((2, 1, cols), F32),
               pltpu.VMEM((nt, 1, cols), F32),
               pltpu.VMEM((nt, vrows, cols), F32)]
    diff = diff_args is not None
    q_spec = pl.BlockSpec((1, nt * tq, q_slabs * LANES), lambda bi, h, i: (bi, i, h))
    k_spec = pl.BlockSpec((1, s, LANES), lambda bi, h, i: (bi, 0, h))
    vt_spec = pl.BlockSpec((vrows, s), lambda bi, h, i: (h, bi))
    in_specs = [q_spec, k_spec, vt_spec]
    args = [q, k, vt]
    out_w = 1 if diff else q_slabs
    if diff:
        assert tq == tk, (tq, tk)
        scratch.append(pltpu.VMEM((3, tk, tq), F32))
        slopes, lq1, lk1, lq2, lk2, subg = diff_args
        in_specs = ([pl.BlockSpec(memory_space=pltpu.SMEM)] + in_specs
                    + [_const_spec((1, C_HEAD_DIM))] * 4 + [_const_spec((LANES, 1))])
        args = [slopes] + args + [lq1, lk1, lq2, lk2, subg]
    return pl.pallas_call(
        functools.partial(_attn_body, groups=groups, packed=packed, tq=tq, tk=tk, nt=nt, seq=s,
                          diff=diff, lambda_init=lambda_init),
        grid=(b, nkv, s // (nt * tq)),
        in_specs=in_specs,
        out_specs=pl.BlockSpec((1, nt * tq, out_w * LANES), lambda bi, h, i: (bi, i, h)),
        out_shape=jax.ShapeDtypeStruct((b, s, nkv * out_w * LANES), BF16),
        scratch_shapes=scratch,
        compiler_params=_cparams(3),
        name="attn_diff" if diff else f"attn_g{groups}",
    )(*args)


def _pad_slabs(w, n, width, offset=0):
    kdim = w.shape[0]
    w = w.reshape(kdim, n, width)
    w = jnp.pad(w, ((0, 0), (0, 0), (offset, LANES - width - offset)))
    return w.reshape(kdim, n * LANES)


def _rot_cols(w, n, perm, sign):
    kdim = w.shape[0]
    width = perm.shape[0]
    w = w.reshape(kdim, n, width)[:, :, perm] * sign
    return w.reshape(kdim, n * width)


_HALF = B_ROPE // 2
_PERM32 = np.concatenate([np.arange(_HALF, 2 * _HALF), np.arange(_HALF)])
_SIGN32 = np.concatenate([-np.ones(_HALF), np.ones(_HALF)]).astype(np.float32)
_PERM64 = np.concatenate([_PERM32, _PERM32 + 2 * _HALF])
_SIGN64 = np.concatenate([_SIGN32, _SIGN32])


def _rope_tables(seq):
    t = jnp.arange(seq, dtype=jnp.int32)
    inv = ROPE_BASE ** (-jnp.arange(_HALF, dtype=F32) * (2.0 / (2 * _HALF)))

    def cs(pos):
        ang = pos.astype(F32)[:, None] * inv[None, :]
        return jnp.cos(ang), jnp.sin(ang)

    cr, sr = cs(t // GRID_W)
    cc, sc = cs(t % GRID_W)
    ct, st = cs(t)
    cosa = jnp.concatenate([cr, cr, cc, cc] * (LANES // A_HEAD_DIM), axis=1)
    sina = jnp.concatenate([sr, sr, sc, sc] * (LANES // A_HEAD_DIM), axis=1)
    zb = jnp.zeros((seq, LANES - B_NOPE - B_ROPE), F32)
    cosb = jnp.concatenate([jnp.ones((seq, B_NOPE), F32), ct, ct, zb], axis=1)
    sinb = jnp.concatenate([jnp.zeros((seq, B_NOPE), F32), st, st, zb], axis=1)
    return cosa, sina, cosb, sinb


def _vt_weights(w, dv):
    kdim, n, _ = w.shape
    w = jnp.pad(w, ((0, 0), (0, 0), (0, ONES_PAD)))
    return w.reshape(kdim, n * (dv + ONES_PAD)).T.astype(BF16)


def _lane_row(g, perm=None):
    g = g.astype(F32)
    if perm is not None:
        g = g[perm]
    return jnp.tile(g, LANES // g.shape[0]).reshape(1, LANES)


def _prep_even(w_in, a_qn, a_kn, b_qn, b_wuq, b_kvn, b_wukv, w_out):
    na, nk = A_HEADS * A_HEAD_DIM, A_KV_HEADS * A_HEAD_DIM
    wq, wk, wv = w_in[:, :na], w_in[:, na:na + nk], w_in[:, na + nk:na + 2 * nk]
    c = na + 2 * nk
    wcq, wckv = w_in[:, c:c + B_Q_RANK], w_in[:, c + B_Q_RANK:c + B_Q_RANK + B_KV_RANK]
    wkr = w_in[:, c + B_Q_RANK + B_KV_RANK:]
    wk_rot = _rot_cols(wk, A_KV_HEADS, _PERM64, _SIGN64)
    twice = lambda w: jnp.tile(w.reshape(D_MODEL, A_KV_HEADS, 1, A_HEAD_DIM),
                               (1, 1, 2, 1)).reshape(D_MODEL, 2 * nk)
    w1 = jnp.concatenate([
        wq, _rot_cols(wq, A_HEADS, _PERM64, _SIGN64),
        twice(wk), twice(wk_rot),
        wcq, wckv,
        _pad_slabs(wkr, 1, B_ROPE, offset=B_NOPE),
        _pad_slabs(_rot_cols(wkr, 1, _PERM32, _SIGN32), 1, B_ROPE, offset=B_NOPE),
    ], axis=1).astype(BF16)
    dq = B_NOPE + B_ROPE
    uq = b_wuq.reshape(B_Q_RANK, B_HEADS, dq)
    uq_rot = uq[:, :, B_NOPE:][:, :, _PERM32] * _SIGN32
    wuq = jnp.concatenate([
        _pad_slabs(uq.reshape(B_Q_RANK, B_HEADS * dq), B_HEADS, dq),
        _pad_slabs(uq_rot.reshape(B_Q_RANK, B_HEADS * B_ROPE), B_HEADS, B_ROPE, offset=B_NOPE),
    ], axis=1).astype(BF16)
    ukv = b_wukv.reshape(B_KV_RANK, B_HEADS, B_NOPE + B_VDIM)
    wukv = _pad_slabs(ukv[:, :, :B_NOPE].reshape(B_KV_RANK, -1), B_HEADS, B_NOPE).astype(BF16)
    wvbt = _vt_weights(ukv[:, :, B_NOPE:], B_VDIM)
    wvat = _vt_weights(wv.reshape(D_MODEL, A_KV_HEADS, A_HEAD_DIM), A_HEAD_DIM)
    wo_a = w_out[:na]
    wo_b = jnp.pad(w_out[na:].reshape(B_HEADS, B_VDIM, D_MODEL),
                   ((0, 0), (0, LANES - B_VDIM), (0, 0))).reshape(B_HEADS * LANES, D_MODEL)
    return dict(
        w1=w1, wvat=wvat, wuq=wuq, wukv=wukv, wvbt=wvbt, wo_a=wo_a.astype(BF16), wo_b=wo_b.astype(BF16),
        gq=_lane_row(a_qn), gqp=_lane_row(a_qn, _PERM64),
        gk=_lane_row(a_kn), gkp=_lane_row(a_kn, _PERM64),
        bqn=b_qn.reshape(1, -1), bkvn=b_kvn.reshape(1, -1))


def _prep_odd(w_in, w_out):
    n = C_HEADS * 2 * C_HEAD_DIM
    w = w_in[:, :2 * n].astype(BF16)
    wvt = _vt_weights(w_in[:, 2 * n:].reshape(D_MODEL, C_HEADS, 2 * C_HEAD_DIM), 2 * C_HEAD_DIM)
    return dict(w=w, wvt=wvt, wo=w_out.astype(BF16))


def _alibi_slopes():
    return 2.0 ** (-8.0 * jnp.arange(1, C_HEADS + 1, dtype=F32) / C_HEADS)


def _trunk(x, p, lw, final_norm, tm=512,
           tiles_a=(128, 256, 64), tiles_b=(512, 256, 64), tiles_c=(256, 256, 64)):
    b, s, d = x.shape
    t = b * s
    tm_ffn = min(tm, s)
    x = x.reshape(t, d)
    p_all = p.reshape(DEPTH * t, p.shape[-1])
    tabs = _rope_tables(s)
    slopes = _alibi_slopes()
    fg = final_norm.reshape(1, d)
    for i in range(DEPTH):
        w = lw[i]
        ffn1 = (w["ffn1_norm"], w["ffn1_wg"], w["ffn1_wu"], w["ffn1_wd"])
        if i % 2 == 0:
            m = w["mix"]
            x, qa, ka, vat, qb, kb, vbt = _pre_even(
                x, ffn1, w["mix_norm"], m["w1"], m["wvat"], tabs, m["gq"], m["gqp"], m["gk"],
                m["gkp"], m["bqn"], m["wuq"], m["bkvn"], m["wukv"], m["wvbt"], s, tm_ffn)
            r3 = lambda a: a.reshape(b, s, a.shape[1])
            oa = _attention(r3(qa), r3(ka), vat, A_HEADS // A_KV_HEADS, tiles_a, packed=True)
            ob = _attention(r3(qb), r3(kb), vbt, 1, tiles_b)
            mixed = [(oa.reshape(t, -1), m["wo_a"]), (ob.reshape(t, -1), m["wo_b"])]
        else:
            m = w["mix"]
            lambda_init = 0.8 - 0.6 * math.exp(-0.3 * i)
            x, q, k, vt = _pre_odd(x, ffn1, w["mix_norm"], m["w"], m["wvt"], tm_ffn)
            r3 = lambda a: a.reshape(b, s, a.shape[1])
            oc = _attention(r3(q), r3(k), vt, 2, tiles_c, packed=True,
                            diff_args=(slopes, w["lq1"], w["lk1"], w["lq2"], w["lk2"], w["subg"]),
                            lambda_init=lambda_init)
            mixed = [(oc.reshape(t, -1), m["wo"])]
        x = _post(x, mixed, w["ffn2_norm"], w["ffn2_wg"], w["ffn2_wu"], w["ffn2_wd"],
                  p_all, i, w["ple_norm"], w["ple_w_gate"], w["ple_w_proj"], fg,
                  i == DEPTH - 1, tm_ffn)
    return x.reshape(b, s, d)


def _layer_weights(ffn1_norm, ffn1_wg, ffn1_wu, ffn1_wd, mix_norm, ab_w_in, a_q_norm, a_k_norm,
                   b_q_norm, b_w_uq, b_kv_norm, b_w_ukv, ab_w_out, c_w_in, c_lambda_q1,
                   c_lambda_k1, c_lambda_q2, c_lambda_k2, c_sub_norm, c_w_out, ffn2_norm, ffn2_wg,
                   ffn2_wu, ffn2_wd, ple_norm, ple_w_gate, ple_w_proj):
    layers = []
    for i in range(DEPTH):
        w = dict(
            ffn1_norm=ffn1_norm[i].reshape(1, -1), ffn1_wg=ffn1_wg[i].astype(BF16),
            ffn1_wu=ffn1_wu[i].astype(BF16), ffn1_wd=ffn1_wd[i].astype(BF16),
            mix_norm=mix_norm[i].reshape(1, -1),
            ffn2_norm=ffn2_norm[i].reshape(1, -1), ffn2_wg=ffn2_wg[i].astype(BF16),
            ffn2_wu=ffn2_wu[i].astype(BF16), ffn2_wd=ffn2_wd[i].astype(BF16),
            ple_norm=ple_norm[i].reshape(1, -1), ple_w_gate=ple_w_gate[i].astype(BF16),
            ple_w_proj=ple_w_proj[i].astype(BF16))
        j = i // 2
        if i % 2 == 0:
            w["mix"] = _prep_even(ab_w_in[j], a_q_norm[j], a_k_norm[j], b_q_norm[j], b_w_uq[j],
                                  b_kv_norm[j], b_w_ukv[j], ab_w_out[j])
        else:
            w["mix"] = _prep_odd(c_w_in[j], c_w_out[j])
            w["lq1"] = c_lambda_q1[j].reshape(1, -1)
            w["lk1"] = c_lambda_k1[j].reshape(1, -1)
            w["lq2"] = c_lambda_q2[j].reshape(1, -1)
            w["lk2"] = c_lambda_k2[j].reshape(1, -1)
            w["subg"] = c_sub_norm[j].reshape(-1, 1)
        layers.append(w)
    return layers


def kernel(x_prompt, x_sample, p_prompt, p_sample, ffn1_norm, ffn1_wg, ffn1_wu, ffn1_wd, mix_norm, ab_w_in, a_q_norm, a_k_norm, b_q_norm, b_w_uq, b_kv_norm, b_w_ukv, ab_w_out, c_w_in, c_lambda_q1, c_lambda_k1, c_lambda_q2, c_lambda_k2, c_sub_norm, c_w_out, ffn2_norm, ffn2_wg, ffn2_wu, ffn2_wd, ple_norm, ple_w_gate, ple_w_proj, final_norm):
    lw = _layer_weights(ffn1_norm, ffn1_wg, ffn1_wu, ffn1_wd, mix_norm, ab_w_in, a_q_norm,
                        a_k_norm, b_q_norm, b_w_uq, b_kv_norm, b_w_ukv, ab_w_out, c_w_in,
                        c_lambda_q1, c_lambda_k1, c_lambda_q2, c_lambda_k2, c_sub_norm, c_w_out,
                        ffn2_norm, ffn2_wg, ffn2_wu, ffn2_wd, ple_norm, ple_w_gate, ple_w_proj)
    y_prompt = _trunk(x_prompt, p_prompt, lw, final_norm)
    y_sample = _trunk(x_sample, p_sample, lw, final_norm)
    return (y_prompt, y_sample)
```

```python
import functools
import math

import numpy as np
import jax
import jax.numpy as jnp
from jax import lax
from jax.experimental import pallas as pl
from jax.experimental.pallas import tpu as pltpu

D_MODEL = 1024
DEPTH = 4
GRID_W = 64
EPS = 1e-6
D_FF = 2816
D_PLE = 256
ROPE_BASE = 10000.0
A_HEADS = 8
A_KV_HEADS = 2
A_HEAD_DIM = 64
B_HEADS = 8
B_NOPE = 64
B_ROPE = 32
B_VDIM = 64
B_Q_RANK = 256
B_KV_RANK = 128
C_HEADS = 8
C_HEAD_DIM = 64

LANES = 128
VMEM_LIMIT = 56 * 1024 * 1024
NEG_BIG = -1e30
LOG2E = math.log2(math.e)
ONES_PAD = 16

F32 = jnp.float32
BF16 = jnp.bfloat16


def _cparams(n_axes):
    return pltpu.CompilerParams(
        dimension_semantics=("parallel",) * n_axes, vmem_limit_bytes=VMEM_LIMIT)


def _const_spec(shape):
    nd = len(shape)
    return pl.BlockSpec(shape, lambda *_: (0,) * nd, pipeline_mode=pl.Buffered(1))


def _rms(x, g):
    return x * lax.rsqrt(jnp.mean(x * x, axis=-1, keepdims=True) + EPS) * g


def _sigmoid(x):
    return 1.0 / (1.0 + jnp.exp(-x))


def _half_swiglu(x, g_ref, wg_ref, wu_ref, wd_ref):
    xn = _rms(x, g_ref[...]).astype(BF16)
    g = jnp.dot(xn, wg_ref[...], preferred_element_type=F32)
    u = jnp.dot(xn, wu_ref[...], preferred_element_type=F32)
    a = (g * _sigmoid(g) * u).astype(BF16)
    return x + 0.5 * jnp.dot(a, wd_ref[...], preferred_element_type=F32)


def _ffn_specs(ffn):
    return [_const_spec(a.shape) for a in ffn]


def _post_body(*refs, n_pairs, final):
    x_ref = refs[0]
    pairs = refs[1:1 + 2 * n_pairs]
    (g_ref, wg_ref, wu_ref, wd_ref, p_ref, pg_ref, wgate_ref, wproj_ref, fg_ref,
     o_ref) = refs[1 + 2 * n_pairs:]
    x = x_ref[...]
    for a_ref, w_ref in zip(pairs[0::2], pairs[1::2]):
        x = x + jnp.dot(a_ref[...], w_ref[...], preferred_element_type=F32)
    x = _half_swiglu(x, g_ref, wg_ref, wu_ref, wd_ref)
    xn = _rms(x, pg_ref[...]).astype(BF16)
    gate = _sigmoid(jnp.dot(xn, wgate_ref[...], preferred_element_type=F32))
    proj = jnp.dot(p_ref[...].astype(BF16), wproj_ref[...], preferred_element_type=F32)
    y = x + gate * proj
    if final:
        y = _rms(y, fg_ref[...])
    o_ref[...] = y


def _post(x, pairs, g, wg, wu, wd, p, layer, pg, wgate, wproj, fg, final, tm):
    t, d = x.shape
    row = lambda i: (i, 0)
    p_row = lambda i: (layer * (t // tm) + i, 0)
    in_specs = [pl.BlockSpec((tm, d), row)]
    args = [x]
    for a, w in pairs:
        in_specs += [pl.BlockSpec((tm, a.shape[1]), row), _const_spec(w.shape)]
        args += [a, w]
    in_specs += [_const_spec((1, d)), _const_spec(wg.shape), _const_spec(wu.shape),
                 _const_spec(wd.shape), pl.BlockSpec((tm, p.shape[1]), p_row), _const_spec((1, d)),
                 _const_spec(wgate.shape), _const_spec(wproj.shape), _const_spec((1, d))]
    args += [g, wg, wu, wd, p, pg, wgate, wproj, fg]
    return pl.pallas_call(
        functools.partial(_post_body, n_pairs=len(pairs), final=final),
        grid=(t // tm,),
        in_specs=in_specs,
        out_specs=pl.BlockSpec((tm, d), row),
        out_shape=jax.ShapeDtypeStruct((t, d), F32),
        compiler_params=_cparams(1),
        name="post",
    )(*args)


def _slab(ref_or_val, i):
    return ref_or_val[:, LANES * i:LANES * (i + 1)]


def _dot_nt(a, b):
    return lax.dot_general(a, b, (((1,), (1,)), ((), ())), preferred_element_type=F32)


def _values_t(wvt, h, dv):
    vt = _dot_nt(wvt, h)
    row = lax.broadcasted_iota(jnp.int32, vt.shape, 0)
    return jnp.where(row % (dv + ONES_PAD) == dv, 1.0, vt).astype(BF16)


def _pre_even_body(x_ref, fg_ref, fwg_ref, fwu_ref, fwd_ref,
                   g_ref, w1_ref, wvat_ref, cosa_ref, sina_ref, cosb_ref, sinb_ref,
                   gq_ref, gqp_ref, gk_ref, gkp_ref, bqn_ref, wuq_ref, bkvn_ref, wukv_ref,
                   wvbt_ref, xo_ref, qa_ref, ka_ref, vat_ref, qb_ref, kb_ref, vbt_ref):
    x = _half_swiglu(x_ref[...], fg_ref, fwg_ref, fwu_ref, fwd_ref)
    xo_ref[...] = x
    h = _rms(x, g_ref[...]).astype(BF16)
    z = jnp.dot(h, w1_ref[...], preferred_element_type=F32)
    vat_ref[...] = _values_t(wvat_ref[...], h, A_HEAD_DIM)
    cosa, sina = cosa_ref[...], sina_ref[...]
    cosb, sinb = cosb_ref[...], sinb_ref[...]

    scale_a = A_HEAD_DIM ** -0.5 * LOG2E
    cq, sq = cosa * (gq_ref[...] * scale_a), sina * (gqp_ref[...] * scale_a)
    ck, sk = cosa * gk_ref[...], sina * gkp_ref[...]
    low = lax.broadcasted_iota(jnp.int32, (x_ref.shape[0], LANES), 1) < A_HEAD_DIM

    def half_rsqrt(zz):
        sq2 = zz * zz
        lo = jnp.sum(jnp.where(low, sq2, 0.0), axis=-1, keepdims=True)
        hi = jnp.sum(jnp.where(low, 0.0, sq2), axis=-1, keepdims=True)
        return lax.rsqrt(jnp.where(low, lo, hi) * (1.0 / A_HEAD_DIM) + EPS)

    n_q, n_k = A_HEADS // 2, A_KV_HEADS
    for i in range(n_q):
        zz, zr = _slab(z, i), _slab(z, n_q + i)
        qa_ref[:, LANES * i:LANES * (i + 1)] = ((zz * cq + zr * sq) * half_rsqrt(zz)).astype(BF16)
    for i in range(n_k):
        zz, zr = _slab(z, 2 * n_q + i), _slab(z, 2 * n_q + n_k + i)
        ka_ref[:, LANES * i:LANES * (i + 1)] = ((zz * ck + zr * sk) * half_rsqrt(zz)).astype(BF16)

    c0 = LANES * (2 * n_q + 2 * n_k)
    scale_b = (B_NOPE + B_ROPE) ** -0.5 * LOG2E
    cqn = _rms(z[:, c0:c0 + B_Q_RANK], bqn_ref[...]).astype(BF16)
    zq = jnp.dot(cqn, wuq_ref[...], preferred_element_type=F32)
    for i in range(B_HEADS):
        qb_ref[:, LANES * i:LANES * (i + 1)] = (
            (_slab(zq, i) * cosb + _slab(zq, B_HEADS + i) * sinb) * scale_b).astype(BF16)
    c1 = c0 + B_Q_RANK
    ckvn = _rms(z[:, c1:c1 + B_KV_RANK], bkvn_ref[...]).astype(BF16)
    zk = jnp.dot(ckvn, wukv_ref[...], preferred_element_type=F32)
    vbt_ref[...] = _values_t(wvbt_ref[...], ckvn, B_VDIM)
    c2 = c1 + B_KV_RANK
    kr = z[:, c2:c2 + LANES] * cosb + z[:, c2 + LANES:c2 + 2 * LANES] * sinb
    for i in range(B_HEADS):
        kb_ref[:, LANES * i:LANES * (i + 1)] = (_slab(zk, i) + kr).astype(BF16)


def _pre_even(x, ffn, g, w1, wvat, tabs, gq, gqp, gk, gkp, bqn, wuq, bkvn, wukv, wvbt, seq, tm):
    t, d = x.shape
    nt = seq // tm
    tab_spec = pl.BlockSpec((tm, LANES), lambda i: (i % nt, 0))
    row = lambda i: (i, 0)
    col = lambda i: (0, i)
    outs = ((d, row, F32),
            (A_HEADS // 2 * LANES, row, BF16), (A_KV_HEADS * LANES, row, BF16),
            (wvat.shape[0], col, BF16),
            (B_HEADS * LANES, row, BF16), (B_HEADS * LANES, row, BF16),
            (wvbt.shape[0], col, BF16))
    return pl.pallas_call(
        _pre_even_body,
        grid=(t // tm,),
        in_specs=[pl.BlockSpec((tm, d), row)] + _ffn_specs(ffn)
                 + [_const_spec((1, d)), _const_spec(w1.shape), _const_spec(wvat.shape),
                    tab_spec, tab_spec, tab_spec, tab_spec,
                    _const_spec((1, LANES)), _const_spec((1, LANES)),
                    _const_spec((1, LANES)), _const_spec((1, LANES)),
                    _const_spec((1, B_Q_RANK)), _const_spec(wuq.shape),
                    _const_spec((1, B_KV_RANK)), _const_spec(wukv.shape),
                    _const_spec(wvbt.shape)],
        out_specs=[pl.BlockSpec((tm, n) if m is row else (n, tm), m) for n, m, _ in outs],
        out_shape=[jax.ShapeDtypeStruct((t, n) if m is row else (n, t), dt)
                   for n, m, dt in outs],
        compiler_params=_cparams(1),
        name="pre_even",
    )(x, *ffn, g, w1, wvat, *tabs, gq, gqp, gk, gkp, bqn, wuq, bkvn, wukv, wvbt)


def _pre_odd_body(x_ref, fg_ref, fwg_ref, fwu_ref, fwd_ref, g_ref, w_ref, wvt_ref,
                  xo_ref, q_ref, k_ref, vt_ref):
    x = _half_swiglu(x_ref[...], fg_ref, fwg_ref, fwu_ref, fwd_ref)
    xo_ref[...] = x
    h = _rms(x, g_ref[...]).astype(BF16)
    z = jnp.dot(h, w_ref[...], preferred_element_type=F32)
    nq = C_HEADS * LANES
    q_ref[...] = (z[:, :nq] * (C_HEAD_DIM ** -0.5 * LOG2E)).astype(BF16)
    k_ref[...] = z[:, nq:].astype(BF16)
    vt_ref[...] = _values_t(wvt_ref[...], h, 2 * C_HEAD_DIM)


def _pre_odd(x, ffn, g, w, wvt, tm):
    t, d = x.shape
    row = lambda i: (i, 0)
    return pl.pallas_call(
        _pre_odd_body,
        grid=(t // tm,),
        in_specs=[pl.BlockSpec((tm, d), row)] + _ffn_specs(ffn)
                 + [_const_spec((1, d)), _const_spec(w.shape), _const_spec(wvt.shape)],
        out_specs=[pl.BlockSpec((tm, d), row),
                   pl.BlockSpec((tm, C_HEADS * LANES), row),
                   pl.BlockSpec((tm, C_HEADS * LANES), row),
                   pl.BlockSpec((wvt.shape[0], tm), lambda i: (0, i))],
        out_shape=[jax.ShapeDtypeStruct((t, d), F32),
                   jax.ShapeDtypeStruct((t, C_HEADS * LANES), BF16),
                   jax.ShapeDtypeStruct((t, C_HEADS * LANES), BF16),
                   jax.ShapeDtypeStruct((wvt.shape[0], t), BF16)],
        compiler_params=_cparams(1),
        name="pre_odd",
    )(x, *ffn, g, w, wvt)


def _attn_body(*refs, groups, packed, tq, tk, nt, seq, diff, lambda_init):
    if diff:
        (slopes_ref, q_ref, k_ref, vt_ref, lq1_ref, lk1_ref, lq2_ref, lk2_ref, subg_ref, o_ref,
         qs_ref, s_ref, p_ref, alpha_ref, m_ref, acc_ref, bias_ref) = refs
    else:
        q_ref, k_ref, vt_ref, o_ref, qs_ref, s_ref, p_ref, alpha_ref, m_ref, acc_ref = refs
    dv = vt_ref.shape[0] - ONES_PAD
    lane = lax.broadcasted_iota(jnp.int32, (tq, LANES), 1)
    for t in range(nt):
        for g in range(groups):
            if packed:
                qg = q_ref[0, t * tq:(t + 1) * tq, LANES * (g // 2):LANES * (g // 2 + 1)]
                qg = jnp.where((lane >= LANES // 2) == (g % 2 == 1), qg, jnp.zeros_like(qg))
            else:
                qg = q_ref[0, t * tq:(t + 1) * tq, LANES * g:LANES * (g + 1)]
            if diff:
                qs_ref[t, g * tq:(g + 1) * tq, :] = qg
            else:
                qs_ref[t, :, g * tq:(g + 1) * tq] = qg.astype(F32).T.astype(BF16)
    m_ref[...] = jnp.full(m_ref.shape, NEG_BIG, F32)
    acc_ref[...] = jnp.zeros(acc_ref.shape, F32)
    if diff:
        slope = slopes_ref[pl.program_id(1)] * LOG2E
        q_base = pl.program_id(2) * (nt * tq)
        rel = (lax.broadcasted_iota(jnp.int32, (tk, tq), 1)
               - lax.broadcasted_iota(jnp.int32, (tk, tq), 0)).astype(F32)
        bias_ref[0] = slope * rel
        bias_ref[1] = -slope * jnp.abs(rel)
        bias_ref[2] = -slope * rel

    def scores(item, slot):
        t, c = item
        kc = k_ref[0, c * tk:(c + 1) * tk, :]
        if diff:
            s = _dot_nt(kc, qs_ref[t])
        else:
            s = jnp.dot(kc, qs_ref[t], preferred_element_type=F32)
        if diff:
            tile = bias_ref[jnp.sign(q_base + (t * tq - c * tk)) + 1]
            s = s + jnp.concatenate([tile] * groups, axis=1)
        s_ref[slot] = s

    def softmax(item, s_slot, slot):
        t, c = item
        s = s_ref[s_slot]
        m = m_ref[t]
        if diff:
            delta = q_base + (t * tq - c * tk)
            off = slope * jnp.abs(delta).astype(F32)
            m_new = jnp.maximum(m, jnp.max(s, axis=0, keepdims=True) - off)
            shift = m_new + off
        else:
            m_new = jnp.maximum(m, jnp.max(s, axis=0, keepdims=True))
            shift = m_new
        m_ref[t] = m_new
        alpha_ref[slot] = jnp.exp2(m - m_new)
        p_ref[slot] = jnp.exp2(s - shift).astype(BF16)

    def values(item, slot):
        t, c = item
        pv = jnp.dot(vt_ref[:, c * tk:(c + 1) * tk], p_ref[slot], preferred_element_type=F32)
        acc_ref[t] = alpha_ref[slot] * acc_ref[t] + pv

    def finalize(t):
        ot = acc_ref[t, :dv, :] / acc_ref[t, dv:dv + 1, :]
        rows = slice(t * tq, (t + 1) * tq)
        if diff:
            lam = (jnp.exp(jnp.sum(lq1_ref[...] * lk1_ref[...], keepdims=True))
                   - jnp.exp(jnp.sum(lq2_ref[...] * lk2_ref[...], keepdims=True)) + lambda_init)
            od = ot[:, :tq] - lam * ot[:, tq:]
            r = lax.rsqrt(jnp.mean(od * od, axis=0, keepdims=True) + EPS)
            od = od * r * (subg_ref[...] * (1.0 - lambda_init))
            o_ref[0, rows, :] = od.T.astype(BF16)
        elif packed:
            for g in range(0, groups, 2):
                og = jnp.concatenate([ot[:, g * tq:(g + 1) * tq],
                                      ot[:, (g + 1) * tq:(g + 2) * tq]], axis=0)
                o_ref[0, rows, LANES * (g // 2):LANES * (g // 2 + 1)] = og.T.astype(BF16)
        else:
            pad = jnp.zeros((LANES - dv, tq), F32)
            for g in range(groups):
                og = jnp.concatenate([ot[:, g * tq:(g + 1) * tq], pad], axis=0)
                o_ref[0, rows, LANES * g:LANES * (g + 1)] = og.T.astype(BF16)

    n_chunks = seq // tk
    items = [(t, c) for t in range(nt) for c in range(n_chunks)]
    ahead = s_ref.shape[0]
    for u in range(min(ahead, len(items))):
        scores(items[u], u % ahead)
    softmax(items[0], 0, 0)
    for u, item in enumerate(items):
        if u + ahead < len(items):
            scores(items[u + ahead], u % ahead)
        values(item, u % 2)
        if item[1] == n_chunks - 1:
            finalize(item[0])
        if u + 1 < len(items):
            softmax(items[u + 1], (u + 1) % ahead, (u + 1) % 2)


def _attention(q, k, vt, groups, tiles, packed=False, diff_args=None, lambda_init=0.0):
    b, s, _ = q.shape
    nkv = k.shape[2] // LANES
    q_slabs = groups // 2 if packed else groups
    tq, tk, items = tiles
    tq, tk = min(tq, s), min(tk, s)
    nt = max(1, min(s // tq, items // (s // tk)))
    assert s % tk == 0 and s // tk >= 2 and s % (nt * tq) == 0, (s, tiles)
    cols = groups * tq
    vrows = vt.shape[0] // nkv
    diff = diff_args is not None
    qs_shape = (nt, cols, LANES) if diff else (nt, LANES, cols)
    scratch = [pltpu.VMEM(qs_shape, BF16),
               pltpu.VMEM((2, tk, cols), F32),
               pltpu.VMEM((2, tk, cols), BF16),
               pltpu.VMEM((2, 1, cols), F32),
               pltpu.VMEM((nt, 1, cols), F32),
               pltpu.VMEM((nt, vrows, cols), F32)]
    q_spec = pl.BlockSpec((1, nt * tq, q_slabs * LANES), lambda bi, h, i: (bi, i, h))
    k_spec = pl.BlockSpec((1, s, LANES), lambda bi, h, i: (bi, 0, h))
    vt_spec = pl.BlockSpec((vrows, s), lambda bi, h, i: (h, bi))
    in_specs = [q_spec, k_spec, vt_spec]
    args = [q, k, vt]
    out_w = 1 if diff else q_slabs
    if diff:
        assert tq == tk, (tq, tk)
        scratch.append(pltpu.VMEM((3, tk, tq), F32))
        slopes, lq1, lk1, lq2, lk2, subg = diff_args
        in_specs = ([pl.BlockSpec(memory_space=pltpu.SMEM)] + in_specs
                    + [_const_spec((1, C_HEAD_DIM))] * 4 + [_const_spec((LANES, 1))])
        args = [slopes] + args + [lq1, lk1, lq2, lk2, subg]
    return pl.pallas_call(
        functools.partial(_attn_body, groups=groups, packed=packed, tq=tq, tk=tk, nt=nt, seq=s,
                          diff=diff, lambda_init=lambda_init),
        grid=(b, nkv, s // (nt * tq)),
        in_specs=in_specs,
        out_specs=pl.BlockSpec((1, nt * tq, out_w * LANES), lambda bi, h, i: (bi, i, h)),
        out_shape=jax.ShapeDtypeStruct((b, s, nkv * out_w * LANES), BF16),
        scratch_shapes=scratch,
        compiler_params=_cparams(3),
        name="attn_diff" if diff else f"attn_g{groups}",
    )(*args)


def _pad_slabs(w, n, width, offset=0):
    kdim = w.shape[0]
    w = w.reshape(kdim, n, width)
    w = jnp.pad(w, ((0, 0), (0, 0), (offset, LANES - width - offset)))
    return w.reshape(kdim, n * LANES)


def _rot_cols(w, n, perm, sign):
    kdim = w.shape[0]
    width = perm.shape[0]
    w = w.reshape(kdim, n, width)[:, :, perm] * sign
    return w.reshape(kdim, n * width)


_HALF = B_ROPE // 2
_PERM32 = np.concatenate([np.arange(_HALF, 2 * _HALF), np.arange(_HALF)])
_SIGN32 = np.concatenate([-np.ones(_HALF), np.ones(_HALF)]).astype(np.float32)
_PERM64 = np.concatenate([_PERM32, _PERM32 + 2 * _HALF])
_SIGN64 = np.concatenate([_SIGN32, _SIGN32])


def _rope_tables(seq):
    t = jnp.arange(seq, dtype=jnp.int32)
    inv = ROPE_BASE ** (-jnp.arange(_HALF, dtype=F32) * (2.0 / (2 * _HALF)))

    def cs(pos):
        ang = pos.astype(F32)[:, None] * inv[None, :]
        return jnp.cos(ang), jnp.sin(ang)

    cr, sr = cs(t // GRID_W)
    cc, sc = cs(t % GRID_W)
    ct, st = cs(t)
    cosa = jnp.concatenate([cr, cr, cc, cc] * (LANES // A_HEAD_DIM), axis=1)
    sina = jnp.concatenate([sr, sr, sc, sc] * (LANES // A_HEAD_DIM), axis=1)
    zb = jnp.zeros((seq, LANES - B_NOPE - B_ROPE), F32)
    cosb = jnp.concatenate([jnp.ones((seq, B_NOPE), F32), ct, ct, zb], axis=1)
    sinb = jnp.concatenate([jnp.zeros((seq, B_NOPE), F32), st, st, zb], axis=1)
    return cosa, sina, cosb, sinb


def _vt_weights(w, dv):
    kdim, n, _ = w.shape
    w = jnp.pad(w, ((0, 0), (0, 0), (0, ONES_PAD)))
    return w.reshape(kdim, n * (dv + ONES_PAD)).T.astype(BF16)


def _lane_row(g, perm=None):
    g = g.astype(F32)
    if perm is not None:
        g = g[perm]
    return jnp.tile(g, LANES // g.shape[0]).reshape(1, LANES)


def _prep_even(w_in, a_qn, a_kn, b_qn, b_wuq, b_kvn, b_wukv, w_out):
    na, nk = A_HEADS * A_HEAD_DIM, A_KV_HEADS * A_HEAD_DIM
    wq, wk, wv = w_in[:, :na], w_in[:, na:na + nk], w_in[:, na + nk:na + 2 * nk]
    c = na + 2 * nk
    wcq, wckv = w_in[:, c:c + B_Q_RANK], w_in[:, c + B_Q_RANK:c + B_Q_RANK + B_KV_RANK]
    wkr = w_in[:, c + B_Q_RANK + B_KV_RANK:]
    wk_rot = _rot_cols(wk, A_KV_HEADS, _PERM64, _SIGN64)
    twice = lambda w: jnp.tile(w.reshape(D_MODEL, A_KV_HEADS, 1, A_HEAD_DIM),
                               (1, 1, 2, 1)).reshape(D_MODEL, 2 * nk)
    w1 = jnp.concatenate([
        wq, _rot_cols(wq, A_HEADS, _PERM64, _SIGN64),
        twice(wk), twice(wk_rot),
        wcq, wckv,
        _pad_slabs(wkr, 1, B_ROPE, offset=B_NOPE),
        _pad_slabs(_rot_cols(wkr, 1, _PERM32, _SIGN32), 1, B_ROPE, offset=B_NOPE),
    ], axis=1).astype(BF16)
    dq = B_NOPE + B_ROPE
    uq = b_wuq.reshape(B_Q_RANK, B_HEADS, dq)
    uq_rot = uq[:, :, B_NOPE:][:, :, _PERM32] * _SIGN32
    wuq = jnp.concatenate([
        _pad_slabs(uq.reshape(B_Q_RANK, B_HEADS * dq), B_HEADS, dq),
        _pad_slabs(uq_rot.reshape(B_Q_RANK, B_HEADS * B_ROPE), B_HEADS, B_ROPE, offset=B_NOPE),
    ], axis=1).astype(BF16)
    ukv = b_wukv.reshape(B_KV_RANK, B_HEADS, B_NOPE + B_VDIM)
    wukv = _pad_slabs(ukv[:, :, :B_NOPE].reshape(B_KV_RANK, -1), B_HEADS, B_NOPE).astype(BF16)
    wvbt = _vt_weights(ukv[:, :, B_NOPE:], B_VDIM)
    wvat = _vt_weights(wv.reshape(D_MODEL, A_KV_HEADS, A_HEAD_DIM), A_HEAD_DIM)
    wo_a = w_out[:na]
    wo_b = jnp.pad(w_out[na:].reshape(B_HEADS, B_VDIM, D_MODEL),
                   ((0, 0), (0, LANES - B_VDIM), (0, 0))).reshape(B_HEADS * LANES, D_MODEL)
    return dict(
        w1=w1, wvat=wvat, wuq=wuq, wukv=wukv, wvbt=wvbt, wo_a=wo_a.astype(BF16), wo_b=wo_b.astype(BF16),
        gq=_lane_row(a_qn), gqp=_lane_row(a_qn, _PERM64),
        gk=_lane_row(a_kn), gkp=_lane_row(a_kn, _PERM64),
        bqn=b_qn.reshape(1, -1), bkvn=b_kvn.reshape(1, -1))


def _prep_odd(w_in, w_out):
    n = C_HEADS * 2 * C_HEAD_DIM
    w = w_in[:, :2 * n].astype(BF16)
    wvt = _vt_weights(w_in[:, 2 * n:].reshape(D_MODEL, C_HEADS, 2 * C_HEAD_DIM), 2 * C_HEAD_DIM)
    return dict(w=w, wvt=wvt, wo=w_out.astype(BF16))


def _alibi_slopes():
    return 2.0 ** (-8.0 * jnp.arange(1, C_HEADS + 1, dtype=F32) / C_HEADS)


def _trunk(x, p, lw, final_norm, tm=512,
           tiles_a=(128, 256, 64), tiles_b=(512, 256, 64), tiles_c=(256, 256, 64)):
    b, s, d = x.shape
    t = b * s
    tm_ffn = min(tm, s)
    x = x.reshape(t, d)
    p_all = p.reshape(DEPTH * t, p.shape[-1])
    tabs = _rope_tables(s)
    slopes = _alibi_slopes()
    fg = final_norm.reshape(1, d)
    for i in range(DEPTH):
        w = lw[i]
        ffn1 = (w["ffn1_norm"], w["ffn1_wg"], w["ffn1_wu"], w["ffn1_wd"])
        if i % 2 == 0:
            m = w["mix"]
            x, qa, ka, vat, qb, kb, vbt = _pre_even(
                x, ffn1, w["mix_norm"], m["w1"], m["wvat"], tabs, m["gq"], m["gqp"], m["gk"],
                m["gkp"], m["bqn"], m["wuq"], m["bkvn"], m["wukv"], m["wvbt"], s, tm_ffn)
            r3 = lambda a: a.reshape(b, s, a.shape[1])
            oa = _attention(r3(qa), r3(ka), vat, A_HEADS // A_KV_HEADS, tiles_a, packed=True)
            ob = _attention(r3(qb), r3(kb), vbt, 1, tiles_b)
            mixed = [(oa.reshape(t, -1), m["wo_a"]), (ob.reshape(t, -1), m["wo_b"])]
        else:
            m = w["mix"]
            lambda_init = 0.8 - 0.6 * math.exp(-0.3 * i)
            x, q, k, vt = _pre_odd(x, ffn1, w["mix_norm"], m["w"], m["wvt"], tm_ffn)
            r3 = lambda a: a.reshape(b, s, a.shape[1])
            oc = _attention(r3(q), r3(k), vt, 2, tiles_c, packed=True,
                            diff_args=(slopes, w["lq1"], w["lk1"], w["lq2"], w["lk2"], w["subg"]),
                            lambda_init=lambda_init)
            mixed = [(oc.reshape(t, -1), m["wo"])]
        x = _post(x, mixed, w["ffn2_norm"], w["ffn2_wg"], w["ffn2_wu"], w["ffn2_wd"],
                  p_all, i, w["ple_norm"], w["ple_w_gate"], w["ple_w_proj"], fg,
                  i == DEPTH - 1, tm_ffn)
    return x.reshape(b, s, d)


def _layer_weights(ffn1_norm, ffn1_wg, ffn1_wu, ffn1_wd, mix_norm, ab_w_in, a_q_norm, a_k_norm,
                   b_q_norm, b_w_uq, b_kv_norm, b_w_ukv, ab_w_out, c_w_in, c_lambda_q1,
                   c_lambda_k1, c_lambda_q2, c_lambda_k2, c_sub_norm, c_w_out, ffn2_norm, ffn2_wg,
                   ffn2_wu, ffn2_wd, ple_norm, ple_w_gate, ple_w_proj):
    layers = []
    for i in range(DEPTH):
        w = dict(
            ffn1_norm=ffn1_norm[i].reshape(1, -1), ffn1_wg=ffn1_wg[i].astype(BF16),
            ffn1_wu=ffn1_wu[i].astype(BF16), ffn1_wd=ffn1_wd[i].astype(BF16),
            mix_norm=mix_norm[i].reshape(1, -1),
            ffn2_norm=ffn2_norm[i].reshape(1, -1), ffn2_wg=ffn2_wg[i].astype(BF16),
            ffn2_wu=ffn2_wu[i].astype(BF16), ffn2_wd=ffn2_wd[i].astype(BF16),
            ple_norm=ple_norm[i].reshape(1, -1), ple_w_gate=ple_w_gate[i].astype(BF16),
            ple_w_proj=ple_w_proj[i].astype(BF16))
        j = i // 2
        if i % 2 == 0:
            w["mix"] = _prep_even(ab_w_in[j], a_q_norm[j], a_k_norm[j], b_q_norm[j], b_w_uq[j],
                                  b_kv_norm[j], b_w_ukv[j], ab_w_out[j])
        else:
            w["mix"] = _prep_odd(c_w_in[j], c_w_out[j])
            w["lq1"] = c_lambda_q1[j].reshape(1, -1)
            w["lk1"] = c_lambda_k1[j].reshape(1, -1)
            w["lq2"] = c_lambda_q2[j].reshape(1, -1)
            w["lk2"] = c_lambda_k2[j].reshape(1, -1)
            w["subg"] = c_sub_norm[j].reshape(-1, 1)
        layers.append(w)
    return layers


def kernel(x_prompt, x_sample, p_prompt, p_sample, ffn1_norm, ffn1_wg, ffn1_wu, ffn1_wd, mix_norm, ab_w_in, a_q_norm, a_k_norm, b_q_norm, b_w_uq, b_kv_norm, b_w_ukv, ab_w_out, c_w_in, c_lambda_q1, c_lambda_k1, c_lambda_q2, c_lambda_k2, c_sub_norm, c_w_out, ffn2_norm, ffn2_wg, ffn2_wu, ffn2_wd, ple_norm, ple_w_gate, ple_w_proj, final_norm):
    lw = _layer_weights(ffn1_norm, ffn1_wg, ffn1_wu, ffn1_wd, mix_norm, ab_w_in, a_q_norm,
                        a_k_norm, b_q_norm, b_w_uq, b_kv_norm, b_w_ukv, ab_w_out, c_w_in,
                        c_lambda_q1, c_lambda_k1, c_lambda_q2, c_lambda_k2, c_sub_norm, c_w_out,
                        ffn2_norm, ffn2_wg, ffn2_wu, ffn2_wd, ple_norm, ple_w_gate, ple_w_proj)
    y_prompt = _trunk(x_prompt, p_prompt, lw, final_norm)
    y_sample = _trunk(x_sample, p_sample, lw, final_norm)
    return (y_prompt, y_sample)
```

```python
import functools
import math

import numpy as np
import jax
import jax.numpy as jnp
from jax import lax
from jax.experimental import pallas as pl
from jax.experimental.pallas import tpu as pltpu

D_MODEL = 1024
DEPTH = 4
GRID_W = 64
EPS = 1e-6
D_FF = 2816
D_PLE = 256
ROPE_BASE = 10000.0
A_HEADS = 8
A_KV_HEADS = 2
A_HEAD_DIM = 64
B_HEADS = 8
B_NOPE = 64
B_ROPE = 32
B_VDIM = 64
B_Q_RANK = 256
B_KV_RANK = 128
C_HEADS = 8
C_HEAD_DIM = 64

LANES = 128
VMEM_LIMIT = 56 * 1024 * 1024
NEG_BIG = -1e30
LOG2E = math.log2(math.e)
ONES_PAD = 16

F32 = jnp.float32
BF16 = jnp.bfloat16


def _cparams(n_axes, fuse_inputs=0):
    return pltpu.CompilerParams(
        dimension_semantics=("parallel",) * n_axes, vmem_limit_bytes=VMEM_LIMIT,
        allow_input_fusion=[True] * fuse_inputs if fuse_inputs else None)


def _const_spec(shape):
    nd = len(shape)
    return pl.BlockSpec(shape, lambda *_: (0,) * nd, pipeline_mode=pl.Buffered(1))


def _rms(x, g):
    return x * lax.rsqrt(jnp.mean(x * x, axis=-1, keepdims=True) + EPS) * g


def _sigmoid(x):
    return 1.0 / (1.0 + jnp.exp(-x))


def _half_swiglu(x, g_ref, wg_ref, wu_ref, wd_ref):
    xn = _rms(x, g_ref[...]).astype(BF16)
    g = jnp.dot(xn, wg_ref[...], preferred_element_type=F32)
    u = jnp.dot(xn, wu_ref[...], preferred_element_type=F32)
    a = (g * _sigmoid(g) * u).astype(BF16)
    return x + 0.5 * jnp.dot(a, wd_ref[...], preferred_element_type=F32)


def _ffn_specs(ffn):
    return [_const_spec(a.shape) for a in ffn]


def _post_body(*refs, n_pairs, final):
    x_ref = refs[0]
    pairs = refs[1:1 + 2 * n_pairs]
    (g_ref, wg_ref, wu_ref, wd_ref, p_ref, pg_ref, wgate_ref, wproj_ref, fg_ref,
     o_ref) = refs[1 + 2 * n_pairs:]
    x = x_ref[...]
    for a_ref, w_ref in zip(pairs[0::2], pairs[1::2]):
        x = x + jnp.dot(a_ref[...], w_ref[...], preferred_element_type=F32)
    x = _half_swiglu(x, g_ref, wg_ref, wu_ref, wd_ref)
    xn = _rms(x, pg_ref[...]).astype(BF16)
    gate = _sigmoid(jnp.dot(xn, wgate_ref[...], preferred_element_type=F32))
    proj = jnp.dot(p_ref[...].astype(BF16), wproj_ref[...], preferred_element_type=F32)
    y = x + gate * proj
    if final:
        y = _rms(y, fg_ref[...])
    o_ref[...] = y


def _post(x, pairs, g, wg, wu, wd, p, layer, pg, wgate, wproj, fg, final, tm):
    t, d = x.shape
    row = lambda i: (i, 0)
    p_row = lambda i: (layer * (t // tm) + i, 0)
    in_specs = [pl.BlockSpec((tm, d), row)]
    args = [x]
    for a, w in pairs:
        in_specs += [pl.BlockSpec((tm, a.shape[1]), row), _const_spec(w.shape)]
        args += [a, w]
    in_specs += [_const_spec((1, d)), _const_spec(wg.shape), _const_spec(wu.shape),
                 _const_spec(wd.shape), pl.BlockSpec((tm, p.shape[1]), p_row), _const_spec((1, d)),
                 _const_spec(wgate.shape), _const_spec(wproj.shape), _const_spec((1, d))]
    args += [g, wg, wu, wd, p, pg, wgate, wproj, fg]
    return pl.pallas_call(
        functools.partial(_post_body, n_pairs=len(pairs), final=final),
        grid=(t // tm,),
        in_specs=in_specs,
        out_specs=pl.BlockSpec((tm, d), row),
        out_shape=jax.ShapeDtypeStruct((t, d), F32),
        compiler_params=_cparams(1, fuse_inputs=len(args)),
        name="post",
    )(*args)


def _slab(ref_or_val, i):
    return ref_or_val[:, LANES * i:LANES * (i + 1)]


def _dot_nt(a, b):
    return lax.dot_general(a, b, (((1,), (1,)), ((), ())), preferred_element_type=F32)


def _values_t(wvt, h, dv):
    vt = _dot_nt(wvt, h)
    row = lax.broadcasted_iota(jnp.int32, vt.shape, 0)
    return jnp.where(row % (dv + ONES_PAD) == dv, 1.0, vt).astype(BF16)


def _pre_even_body(x_ref, fg_ref, fwg_ref, fwu_ref, fwd_ref,
                   g_ref, w1_ref, wvat_ref, cosa_ref, sina_ref, cosb_ref, sinb_ref,
                   gq_ref, gqp_ref, gk_ref, gkp_ref, bqn_ref, wuq_ref, bkvn_ref, wukv_ref,
                   wvbt_ref, xo_ref, qa_ref, ka_ref, vat_ref, qb_ref, kb_ref, vbt_ref):
    x = _half_swiglu(x_ref[...], fg_ref, fwg_ref, fwu_ref, fwd_ref)
    xo_ref[...] = x
    h = _rms(x, g_ref[...]).astype(BF16)
    z = jnp.dot(h, w1_ref[...], preferred_element_type=F32)
    vat_ref[...] = _values_t(wvat_ref[...], h, A_HEAD_DIM)
    cosa, sina = cosa_ref[...], sina_ref[...]
    cosb, sinb = cosb_ref[...], sinb_ref[...]

    scale_a = A_HEAD_DIM ** -0.5 * LOG2E
    cq, sq = cosa * (gq_ref[...] * scale_a), sina * (gqp_ref[...] * scale_a)
    ck, sk = cosa * gk_ref[...], sina * gkp_ref[...]
    low = lax.broadcasted_iota(jnp.int32, (x_ref.shape[0], LANES), 1) < A_HEAD_DIM

    def half_rsqrt(zz):
        sq2 = zz * zz
        lo = jnp.sum(jnp.where(low, sq2, 0.0), axis=-1, keepdims=True)
        hi = jnp.sum(jnp.where(low, 0.0, sq2), axis=-1, keepdims=True)
        return lax.rsqrt(jnp.where(low, lo, hi) * (1.0 / A_HEAD_DIM) + EPS)

    n_q, n_k = A_HEADS // 2, A_KV_HEADS
    for i in range(n_q):
        zz, zr = _slab(z, i), _slab(z, n_q + i)
        qa_ref[:, LANES * i:LANES * (i + 1)] = ((zz * cq + zr * sq) * half_rsqrt(zz)).astype(BF16)
    for i in range(n_k):
        zz, zr = _slab(z, 2 * n_q + i), _slab(z, 2 * n_q + n_k + i)
        ka_ref[:, LANES * i:LANES * (i + 1)] = ((zz * ck + zr * sk) * half_rsqrt(zz)).astype(BF16)

    c0 = LANES * (2 * n_q + 2 * n_k)
    scale_b = (B_NOPE + B_ROPE) ** -0.5 * LOG2E
    cqn = _rms(z[:, c0:c0 + B_Q_RANK], bqn_ref[...]).astype(BF16)
    zq = jnp.dot(cqn, wuq_ref[...], preferred_element_type=F32)
    for i in range(B_HEADS):
        qb_ref[:, LANES * i:LANES * (i + 1)] = (
            (_slab(zq, i) * cosb + _slab(zq, B_HEADS + i) * sinb) * scale_b).astype(BF16)
    c1 = c0 + B_Q_RANK
    ckvn = _rms(z[:, c1:c1 + B_KV_RANK], bkvn_ref[...]).astype(BF16)
    zk = jnp.dot(ckvn, wukv_ref[...], preferred_element_type=F32)
    vbt_ref[...] = _values_t(wvbt_ref[...], ckvn, B_VDIM)
    c2 = c1 + B_KV_RANK
    kr = z[:, c2:c2 + LANES] * cosb + z[:, c2 + LANES:c2 + 2 * LANES] * sinb
    for i in range(B_HEADS):
        kb_ref[:, LANES * i:LANES * (i + 1)] = (_slab(zk, i) + kr).astype(BF16)


def _pre_even(x, ffn, g, w1, wvat, tabs, gq, gqp, gk, gkp, bqn, wuq, bkvn, wukv, wvbt, seq, tm):
    t, d = x.shape
    nt = seq // tm
    tab_spec = pl.BlockSpec((tm, LANES), lambda i: (i % nt, 0))
    row = lambda i: (i, 0)
    col = lambda i: (0, i)
    outs = ((d, row, F32),
            (A_HEADS // 2 * LANES, row, BF16), (A_KV_HEADS * LANES, row, BF16),
            (wvat.shape[0], col, BF16),
            (B_HEADS * LANES, row, BF16), (B_HEADS * LANES, row, BF16),
            (wvbt.shape[0], col, BF16))
    return pl.pallas_call(
        _pre_even_body,
        grid=(t // tm,),
        in_specs=[pl.BlockSpec((tm, d), row)] + _ffn_specs(ffn)
                 + [_const_spec((1, d)), _const_spec(w1.shape), _const_spec(wvat.shape),
                    tab_spec, tab_spec, tab_spec, tab_spec,
                    _const_spec((1, LANES)), _const_spec((1, LANES)),
                    _const_spec((1, LANES)), _const_spec((1, LANES)),
                    _const_spec((1, B_Q_RANK)), _const_spec(wuq.shape),
                    _const_spec((1, B_KV_RANK)), _const_spec(wukv.shape),
                    _const_spec(wvbt.shape)],
        out_specs=[pl.BlockSpec((tm, n) if m is row else (n, tm), m) for n, m, _ in outs],
        out_shape=[jax.ShapeDtypeStruct((t, n) if m is row else (n, t), dt)
                   for n, m, dt in outs],
        compiler_params=_cparams(1, fuse_inputs=21),
        name="pre_even",
    )(x, *ffn, g, w1, wvat, *tabs, gq, gqp, gk, gkp, bqn, wuq, bkvn, wukv, wvbt)


def _pre_odd_body(x_ref, fg_ref, fwg_ref, fwu_ref, fwd_ref, g_ref, w_ref, wvt_ref,
                  xo_ref, q_ref, k_ref, vt_ref):
    x = _half_swiglu(x_ref[...], fg_ref, fwg_ref, fwu_ref, fwd_ref)
    xo_ref[...] = x
    h = _rms(x, g_ref[...]).astype(BF16)
    z = jnp.dot(h, w_ref[...], preferred_element_type=F32)
    nq = C_HEADS * LANES
    q_ref[...] = (z[:, :nq] * (C_HEAD_DIM ** -0.5 * LOG2E)).astype(BF16)
    k_ref[...] = z[:, nq:].astype(BF16)
    vt_ref[...] = _values_t(wvt_ref[...], h, 2 * C_HEAD_DIM)


def _pre_odd(x, ffn, g, w, wvt, tm):
    t, d = x.shape
    row = lambda i: (i, 0)
    return pl.pallas_call(
        _pre_odd_body,
        grid=(t // tm,),
        in_specs=[pl.BlockSpec((tm, d), row)] + _ffn_specs(ffn)
                 + [_const_spec((1, d)), _const_spec(w.shape), _const_spec(wvt.shape)],
        out_specs=[pl.BlockSpec((tm, d), row),
                   pl.BlockSpec((tm, C_HEADS * LANES), row),
                   pl.BlockSpec((tm, C_HEADS * LANES), row),
                   pl.BlockSpec((wvt.shape[0], tm), lambda i: (0, i))],
        out_shape=[jax.ShapeDtypeStruct((t, d), F32),
                   jax.ShapeDtypeStruct((t, C_HEADS * LANES), BF16),
                   jax.ShapeDtypeStruct((t, C_HEADS * LANES), BF16),
                   jax.ShapeDtypeStruct((wvt.shape[0], t), BF16)],
        compiler_params=_cparams(1, fuse_inputs=8),
        name="pre_odd",
    )(x, *ffn, g, w, wvt)


def _attn_body(*refs, groups, packed, tq, tk, nt, seq, diff, lambda_init):
    if diff:
        (slopes_ref, q_ref, k_ref, vt_ref, lq1_ref, lk1_ref, lq2_ref, lk2_ref, subg_ref, o_ref,
         qs_ref, s_ref, p_ref, alpha_ref, m_ref, acc_ref, bias_ref) = refs
    else:
        q_ref, k_ref, vt_ref, o_ref, qs_ref, s_ref, p_ref, alpha_ref, m_ref, acc_ref = refs
    dv = vt_ref.shape[0] - ONES_PAD
    lane = lax.broadcasted_iota(jnp.int32, (tq, LANES), 1)
    for t in range(nt):
        for g in range(groups):
            if packed:
                qg = q_ref[0, t * tq:(t + 1) * tq, LANES * (g // 2):LANES * (g // 2 + 1)]
                qg = jnp.where((lane >= LANES // 2) == (g % 2 == 1), qg, jnp.zeros_like(qg))
            else:
                qg = q_ref[0, t * tq:(t + 1) * tq, LANES * g:LANES * (g + 1)]
            qs_ref[t, g * tq:(g + 1) * tq, :] = qg
    m_ref[...] = jnp.full(m_ref.shape, NEG_BIG, F32)
    acc_ref[...] = jnp.zeros(acc_ref.shape, F32)
    if diff:
        slope = slopes_ref[pl.program_id(1)] * LOG2E
        q_base = pl.program_id(2) * (nt * tq)
        rel = (lax.broadcasted_iota(jnp.int32, (tk, tq), 1)
               - lax.broadcasted_iota(jnp.int32, (tk, tq), 0)).astype(F32)
        bias_ref[0] = slope * rel
        bias_ref[1] = -slope * jnp.abs(rel)
        bias_ref[2] = -slope * rel

    def scores(item, slot):
        t, c = item
        s = _dot_nt(k_ref[0, c * tk:(c + 1) * tk, :], qs_ref[t])
        if diff:
            tile = bias_ref[jnp.sign(q_base + (t * tq - c * tk)) + 1]
            s = s + jnp.concatenate([tile] * groups, axis=1)
        s_ref[slot] = s

    def softmax(item, s_slot, slot):
        t, c = item
        s = s_ref[s_slot]
        m = m_ref[t]
        if diff:
            delta = q_base + (t * tq - c * tk)
            off = slope * jnp.abs(delta).astype(F32)
            m_new = jnp.maximum(m, jnp.max(s, axis=0, keepdims=True) - off)
            shift = m_new + off
        else:
            m_new = jnp.maximum(m, jnp.max(s, axis=0, keepdims=True))
            shift = m_new
        m_ref[t] = m_new
        alpha_ref[slot] = jnp.exp2(m - m_new)
        p_ref[slot] = jnp.exp2(s - shift).astype(BF16)

    def values(item, slot):
        t, c = item
        pv = jnp.dot(vt_ref[:, c * tk:(c + 1) * tk], p_ref[slot], preferred_element_type=F32)
        acc_ref[t] = alpha_ref[slot] * acc_ref[t] + pv

    def finalize(t):
        ot = acc_ref[t, :dv, :] / acc_ref[t, dv:dv + 1, :]
        rows = slice(t * tq, (t + 1) * tq)
        if diff:
            lam = (jnp.exp(jnp.sum(lq1_ref[...] * lk1_ref[...], keepdims=True))
                   - jnp.exp(jnp.sum(lq2_ref[...] * lk2_ref[...], keepdims=True)) + lambda_init)
            od = ot[:, :tq] - lam * ot[:, tq:]
            r = lax.rsqrt(jnp.mean(od * od, axis=0, keepdims=True) + EPS)
            od = od * r * (subg_ref[...] * (1.0 - lambda_init))
            o_ref[0, rows, :] = od.T.astype(BF16)
        elif packed:
            for g in range(0, groups, 2):
                og = jnp.concatenate([ot[:, g * tq:(g + 1) * tq],
                                      ot[:, (g + 1) * tq:(g + 2) * tq]], axis=0)
                o_ref[0, rows, LANES * (g // 2):LANES * (g // 2 + 1)] = og.T.astype(BF16)
        else:
            pad = jnp.zeros((LANES - dv, tq), F32)
            for g in range(groups):
                og = jnp.concatenate([ot[:, g * tq:(g + 1) * tq], pad], axis=0)
                o_ref[0, rows, LANES * g:LANES * (g + 1)] = og.T.astype(BF16)

    n_chunks = seq // tk
    items = [(t, c) for t in range(nt) for c in range(n_chunks)]
    ahead = s_ref.shape[0]
    for u in range(min(ahead, len(items))):
        scores(items[u], u % ahead)
    softmax(items[0], 0, 0)
    for u, item in enumerate(items):
        if u + ahead < len(items):
            scores(items[u + ahead], u % ahead)
        values(item, u % 2)
        if item[1] == n_chunks - 1:
            finalize(item[0])
        if u + 1 < len(items):
            softmax(items[u + 1], (u + 1) % ahead, (u + 1) % 2)


def _attention(q, k, vt, groups, tiles, packed=False, diff_args=None, lambda_init=0.0):
    b, s, _ = q.shape
    nkv = k.shape[2] // LANES
    q_slabs = groups // 2 if packed else groups
    tq, tk, items = tiles
    tq, tk = min(tq, s), min(tk, s)
    nt = max(1, min(s // tq, items // (s // tk)))
    assert s % tk == 0 and s // tk >= 2 and s % (nt * tq) == 0, (s, tiles)
    cols = groups * tq
    vrows = vt.shape[0] // nkv
    scratch = [pltpu.VMEM((nt, cols, LANES), BF16),
               pltpu.VMEM((2, tk, cols), F32),
               pltpu.VMEM((2, tk, cols), BF16),
               pltpu.VMEM((2, 1, cols), F32),
               pltpu.VMEM((nt, 1, cols), F32),
               pltpu.VMEM((nt, vrows, cols), F32)]
    diff = diff_args is not None
    q_spec = pl.BlockSpec((1, nt * tq, q_slabs * LANES), lambda bi, h, i: (bi, i, h))
    k_spec = pl.BlockSpec((1, s, LANES), lambda bi, h, i: (bi, 0, h))
    vt_spec = pl.BlockSpec((vrows, s), lambda bi, h, i: (h, bi))
    in_specs = [q_spec, k_spec, vt_spec]
    args = [q, k, vt]
    out_w = 1 if diff else q_slabs
    if diff:
        assert tq == tk, (tq, tk)
        scratch.append(pltpu.VMEM((3, tk, tq), F32))
        slopes, lq1, lk1, lq2, lk2, subg = diff_args
        in_specs = ([pl.BlockSpec(memory_space=pltpu.SMEM)] + in_specs
                    + [_const_spec((1, C_HEAD_DIM))] * 4 + [_const_spec((LANES, 1))])
        args = [slopes] + args + [lq1, lk1, lq2, lk2, subg]
    return pl.pallas_call(
        functools.partial(_attn_body, groups=groups, packed=packed, tq=tq, tk=tk, nt=nt, seq=s,
                          diff=diff, lambda_init=lambda_init),
        grid=(b, nkv, s // (nt * tq)),
        in_specs=in_specs,
        out_specs=pl.BlockSpec((1, nt * tq, out_w * LANES), lambda bi, h, i: (bi, i, h)),
        out_shape=jax.ShapeDtypeStruct((b, s, nkv * out_w * LANES), BF16),
        scratch_shapes=scratch,
        compiler_params=_cparams(3),
        name="attn_diff" if diff else f"attn_g{groups}",
    )(*args)


def _pad_slabs(w, n, width, offset=0):
    kdim = w.shape[0]
    w = w.reshape(kdim, n, width)
    w = jnp.pad(w, ((0, 0), (0, 0), (offset, LANES - width - offset)))
    return w.reshape(kdim, n * LANES)


def _rot_cols(w, n, perm, sign):
    kdim = w.shape[0]
    width = perm.shape[0]
    w = w.reshape(kdim, n, width)[:, :, perm] * sign
    return w.reshape(kdim, n * width)


_HALF = B_ROPE // 2
_PERM32 = np.concatenate([np.arange(_HALF, 2 * _HALF), np.arange(_HALF)])
_SIGN32 = np.concatenate([-np.ones(_HALF), np.ones(_HALF)]).astype(np.float32)
_PERM64 = np.concatenate([_PERM32, _PERM32 + 2 * _HALF])
_SIGN64 = np.concatenate([_SIGN32, _SIGN32])


def _rope_tables(seq):
    t = jnp.arange(seq, dtype=jnp.int32)
    inv = ROPE_BASE ** (-jnp.arange(_HALF, dtype=F32) * (2.0 / (2 * _HALF)))

    def cs(pos):
        ang = pos.astype(F32)[:, None] * inv[None, :]
        return jnp.cos(ang), jnp.sin(ang)

    cr, sr = cs(t // GRID_W)
    cc, sc = cs(t % GRID_W)
    ct, st = cs(t)
    cosa = jnp.concatenate([cr, cr, cc, cc] * (LANES // A_HEAD_DIM), axis=1)
    sina = jnp.concatenate([sr, sr, sc, sc] * (LANES // A_HEAD_DIM), axis=1)
    zb = jnp.zeros((seq, LANES - B_NOPE - B_ROPE), F32)
    cosb = jnp.concatenate([jnp.ones((seq, B_NOPE), F32), ct, ct, zb], axis=1)
    sinb = jnp.concatenate([jnp.zeros((seq, B_NOPE), F32), st, st, zb], axis=1)
    return cosa, sina, cosb, sinb


def _vt_weights(w, dv):
    kdim, n, _ = w.shape
    w = jnp.pad(w, ((0, 0), (0, 0), (0, ONES_PAD)))
    return w.reshape(kdim, n * (dv + ONES_PAD)).T.astype(BF16)


def _lane_row(g, perm=None):
    g = g.astype(F32)
    if perm is not None:
        g = g[perm]
    return jnp.tile(g, LANES // g.shape[0]).reshape(1, LANES)


def _prep_even(w_in, a_qn, a_kn, b_qn, b_wuq, b_kvn, b_wukv, w_out):
    na, nk = A_HEADS * A_HEAD_DIM, A_KV_HEADS * A_HEAD_DIM
    wq, wk, wv = w_in[:, :na], w_in[:, na:na + nk], w_in[:, na + nk:na + 2 * nk]
    c = na + 2 * nk
    wcq, wckv = w_in[:, c:c + B_Q_RANK], w_in[:, c + B_Q_RANK:c + B_Q_RANK + B_KV_RANK]
    wkr = w_in[:, c + B_Q_RANK + B_KV_RANK:]
    wk_rot = _rot_cols(wk, A_KV_HEADS, _PERM64, _SIGN64)
    twice = lambda w: jnp.tile(w.reshape(D_MODEL, A_KV_HEADS, 1, A_HEAD_DIM),
                               (1, 1, 2, 1)).reshape(D_MODEL, 2 * nk)
    w1 = jnp.concatenate([
        wq, _rot_cols(wq, A_HEADS, _PERM64, _SIGN64),
        twice(wk), twice(wk_rot),
        wcq, wckv,
        _pad_slabs(wkr, 1, B_ROPE, offset=B_NOPE),
        _pad_slabs(_rot_cols(wkr, 1, _PERM32, _SIGN32), 1, B_ROPE, offset=B_NOPE),
    ], axis=1).astype(BF16)
    dq = B_NOPE + B_ROPE
    uq = b_wuq.reshape(B_Q_RANK, B_HEADS, dq)
    uq_rot = uq[:, :, B_NOPE:][:, :, _PERM32] * _SIGN32
    wuq = jnp.concatenate([
        _pad_slabs(uq.reshape(B_Q_RANK, B_HEADS * dq), B_HEADS, dq),
        _pad_slabs(uq_rot.reshape(B_Q_RANK, B_HEADS * B_ROPE), B_HEADS, B_ROPE, offset=B_NOPE),
    ], axis=1).astype(BF16)
    ukv = b_wukv.reshape(B_KV_RANK, B_HEADS, B_NOPE + B_VDIM)
    wukv = _pad_slabs(ukv[:, :, :B_NOPE].reshape(B_KV_RANK, -1), B_HEADS, B_NOPE).astype(BF16)
    wvbt = _vt_weights(ukv[:, :, B_NOPE:], B_VDIM)
    wvat = _vt_weights(wv.reshape(D_MODEL, A_KV_HEADS, A_HEAD_DIM), A_HEAD_DIM)
    wo_a = w_out[:na]
    wo_b = jnp.pad(w_out[na:].reshape(B_HEADS, B_VDIM, D_MODEL),
                   ((0, 0), (0, LANES - B_VDIM), (0, 0))).reshape(B_HEADS * LANES, D_MODEL)
    return dict(
        w1=w1, wvat=wvat, wuq=wuq, wukv=wukv, wvbt=wvbt, wo_a=wo_a.astype(BF16), wo_b=wo_b.astype(BF16),
        gq=_lane_row(a_qn), gqp=_lane_row(a_qn, _PERM64),
        gk=_lane_row(a_kn), gkp=_lane_row(a_kn, _PERM64),
        bqn=b_qn.reshape(1, -1), bkvn=b_kvn.reshape(1, -1))


def _prep_odd(w_in, w_out):
    n = C_HEADS * 2 * C_HEAD_DIM
    w = w_in[:, :2 * n].astype(BF16)
    wvt = _vt_weights(w_in[:, 2 * n:].reshape(D_MODEL, C_HEADS, 2 * C_HEAD_DIM), 2 * C_HEAD_DIM)
    return dict(w=w, wvt=wvt, wo=w_out.astype(BF16))


def _alibi_slopes():
    return 2.0 ** (-8.0 * jnp.arange(1, C_HEADS + 1, dtype=F32) / C_HEADS)


def _trunk(x, p, lw, final_norm, tm=512,
           tiles_a=(128, 256, 64), tiles_b=(512, 256, 64), tiles_c=(256, 256, 64)):
    b, s, d = x.shape
    t = b * s
    tm_ffn = min(tm, s)
    x = x.reshape(t, d)
    p_all = p.reshape(DEPTH * t, p.shape[-1])
    tabs = _rope_tables(s)
    slopes = _alibi_slopes()
    fg = final_norm.reshape(1, d)
    for i in range(DEPTH):
        w = lw[i]
        ffn1 = (w["ffn1_norm"], w["ffn1_wg"], w["ffn1_wu"], w["ffn1_wd"])
        if i % 2 == 0:
            m = w["mix"]
            x, qa, ka, vat, qb, kb, vbt = _pre_even(
                x, ffn1, w["mix_norm"], m["w1"], m["wvat"], tabs, m["gq"], m["gqp"], m["gk"],
                m["gkp"], m["bqn"], m["wuq"], m["bkvn"], m["wukv"], m["wvbt"], s, tm_ffn)
            r3 = lambda a: a.reshape(b, s, a.shape[1])
            oa = _attention(r3(qa), r3(ka), vat, A_HEADS // A_KV_HEADS, tiles_a, packed=True)
            ob = _attention(r3(qb), r3(kb), vbt, 1, tiles_b)
            mixed = [(oa.reshape(t, -1), m["wo_a"]), (ob.reshape(t, -1), m["wo_b"])]
        else:
            m = w["mix"]
            lambda_init = 0.8 - 0.6 * math.exp(-0.3 * i)
            x, q, k, vt = _pre_odd(x, ffn1, w["mix_norm"], m["w"], m["wvt"], tm_ffn)
            r3 = lambda a: a.reshape(b, s, a.shape[1])
            oc = _attention(r3(q), r3(k), vt, 2, tiles_c, packed=True,
                            diff_args=(slopes, w["lq1"], w["lk1"], w["lq2"], w["lk2"], w["subg"]),
                            lambda_init=lambda_init)
            mixed = [(oc.reshape(t, -1), m["wo"])]
        x = _post(x, mixed, w["ffn2_norm"], w["ffn2_wg"], w["ffn2_wu"], w["ffn2_wd"],
                  p_all, i, w["ple_norm"], w["ple_w_gate"], w["ple_w_proj"], fg,
                  i == DEPTH - 1, tm_ffn)
    return x.reshape(b, s, d)


def _layer_weights(ffn1_norm, ffn1_wg, ffn1_wu, ffn1_wd, mix_norm, ab_w_in, a_q_norm, a_k_norm,
                   b_q_norm, b_w_uq, b_kv_norm, b_w_ukv, ab_w_out, c_w_in, c_lambda_q1,
                   c_lambda_k1, c_lambda_q2, c_lambda_k2, c_sub_norm, c_w_out, ffn2_norm, ffn2_wg,
                   ffn2_wu, ffn2_wd, ple_norm, ple_w_gate, ple_w_proj):
    layers = []
    for i in range(DEPTH):
        w = dict(
            ffn1_norm=ffn1_norm[i].reshape(1, -1), ffn1_wg=ffn1_wg[i].astype(BF16),
            ffn1_wu=ffn1_wu[i].astype(BF16), ffn1_wd=ffn1_wd[i].astype(BF16),
            mix_norm=mix_norm[i].reshape(1, -1),
            ffn2_norm=ffn2_norm[i].reshape(1, -1), ffn2_wg=ffn2_wg[i].astype(BF16),
            ffn2_wu=ffn2_wu[i].astype(BF16), ffn2_wd=ffn2_wd[i].astype(BF16),
            ple_norm=ple_norm[i].reshape(1, -1), ple_w_gate=ple_w_gate[i].astype(BF16),
            ple_w_proj=ple_w_proj[i].astype(BF16))
        j = i // 2
        if i % 2 == 0:
            w["mix"] = _prep_even(ab_w_in[j], a_q_norm[j], a_k_norm[j], b_q_norm[j], b_w_uq[j],
                                  b_kv_norm[j], b_w_ukv[j], ab_w_out[j])
        else:
            w["mix"] = _prep_odd(c_w_in[j], c_w_out[j])
            w["lq1"] = c_lambda_q1[j].reshape(1, -1)
            w["lk1"] = c_lambda_k1[j].reshape(1, -1)
            w["lq2"] = c_lambda_q2[j].reshape(1, -1)
            w["lk2"] = c_lambda_k2[j].reshape(1, -1)
            w["subg"] = c_sub_norm[j].reshape(-1, 1)
        layers.append(w)
    return layers


def kernel(x_prompt, x_sample, p_prompt, p_sample, ffn1_norm, ffn1_wg, ffn1_wu, ffn1_wd, mix_norm, ab_w_in, a_q_norm, a_k_norm, b_q_norm, b_w_uq, b_kv_norm, b_w_ukv, ab_w_out, c_w_in, c_lambda_q1, c_lambda_k1, c_lambda_q2, c_lambda_k2, c_sub_norm, c_w_out, ffn2_norm, ffn2_wg, ffn2_wu, ffn2_wd, ple_norm, ple_w_gate, ple_w_proj, final_norm):
    lw = _layer_weights(ffn1_norm, ffn1_wg, ffn1_wu, ffn1_wd, mix_norm, ab_w_in, a_q_norm,
                        a_k_norm, b_q_norm, b_w_uq, b_kv_norm, b_w_ukv, ab_w_out, c_w_in,
                        c_lambda_q1, c_lambda_k1, c_lambda_q2, c_lambda_k2, c_sub_norm, c_w_out,
                        ffn2_norm, ffn2_wg, ffn2_wu, ffn2_wd, ple_norm, ple_w_gate, ple_w_proj)
    y_prompt = _trunk(x_prompt, p_prompt, lw, final_norm)
    y_sample = _trunk(x_sample, p_sample, lw, final_norm)
    return (y_prompt, y_sample)
```
